```python
import math
import jax, jax.numpy as jnp
from jax import lax
import numpy as np

D_MODEL = 1024
BATCH = 16
SEQ = 2048
DEPTH = 2
DEC_BATCH = 32
DEC_SEQ = 1
PAST_LEN = 16384
PAGE_SIZE = 128

HEAD_DIM = 64
H_FOX = 4
H_DIFF = 4
H_SB = 4
H_MEM = 4
DIFF_DC = HEAD_DIM // 2
N_BRANCH = 4
BRANCH_W = 4 * HEAD_DIM
N_MEM = 256
D_FF = 2816
CONV_W = 3
ROPE_THETA = 500000.0
ROPE_FRACTION = 4
Q_BLOCK = 128
RMS_EPS = 1e-6
ATTN_SCALE = HEAD_DIM ** -0.5
DIFF_SCALE = DIFF_DC ** -0.5
SPLIT_POINTS = [BRANCH_W * i for i in range(1, 11)]
N_IN = 10 * BRANCH_W + H_FOX

kernel_name = 'hybrid_fox_diff_stickbreak_decoder_step'


def _rms_norm(x, g):
    xf = x.astype(jnp.float32)
    y = xf * lax.rsqrt(jnp.mean(xf * xf, axis=-1, keepdims=True) + RMS_EPS)
    return (y * g.astype(jnp.float32)).astype(x.dtype)


def _partial_rope(x, pos):
    d = x.shape[-1]
    rot = d // ROPE_FRACTION
    half = rot // 2
    inv_freq = ROPE_THETA ** (-jnp.arange(half, dtype=jnp.float32) * 2.0 / rot)
    ang = pos.astype(jnp.float32)[:, None] * inv_freq[None, :]
    shape = (1, x.shape[1]) + (1,) * (x.ndim - 3) + (half,)
    cos = jnp.cos(ang).reshape(shape)
    sin = jnp.sin(ang).reshape(shape)
    xf = x.astype(jnp.float32)
    x1 = xf[..., :half]
    x2 = xf[..., half:rot]
    out = jnp.concatenate([x1 * cos - x2 * sin, x2 * cos + x1 * sin, xf[..., rot:]], axis=-1)
    return out.astype(x.dtype)


def _fox_attend(q, eq, qpos, k, v, ek, kpos):
    s = jnp.einsum('bqhd,bkhd->bhqk', q, k, preferred_element_type=jnp.float32) * ATTN_SCALE
    s = s + jnp.transpose(ek, (0, 2, 1))[:, :, None, :] - jnp.transpose(eq, (0, 2, 1))[:, :, :, None]
    causal = kpos[None, :] <= qpos[:, None]
    p = jax.nn.softmax(jnp.where(causal, s, -jnp.inf), axis=-1)
    return jnp.einsum('bhqk,bkhd->bqhd', p.astype(v.dtype), v)


def _diff_attend(q, qpos, k, v, kpos, lam):
    s = jnp.einsum('bqhcd,bkhcd->bhcqk', q, k, preferred_element_type=jnp.float32) * DIFF_SCALE
    causal = kpos[None, :] <= qpos[:, None]
    p = jax.nn.softmax(jnp.where(causal, s, -jnp.inf), axis=-1)
    w = p[:, :, 0] - lam * p[:, :, 1]
    return jnp.einsum('bhqk,bkhd->bqhd', w.astype(v.dtype), v)


def _sb_attend(q, qpos, k, v, kpos):
    z = jnp.einsum('bqhd,bkhd->bhqk', q, k, preferred_element_type=jnp.float32) * ATTN_SCALE
    strict = kpos[None, :] < qpos[:, None]
    log_keep = jnp.where(strict, jax.nn.log_sigmoid(-z), 0.0)
    tail = lax.cumsum(log_keep, axis=3, reverse=True) - log_keep
    a = jnp.where(strict, jnp.exp(jax.nn.log_sigmoid(z) + tail), 0.0)
    return jnp.einsum('bhqk,bkhd->bqhd', a.astype(v.dtype), v)


def _mem_attend(q, mk, mv):
    s = jnp.einsum('bqhd,bmhd->bhqm', q, mk, preferred_element_type=jnp.float32) * ATTN_SCALE
    p = jax.nn.softmax(s, axis=-1)
    return jnp.einsum('bhqm,bmhd->bqhd', p.astype(mv.dtype), mv)


def _sweep_blocks(fn, qargs, qpos):
    b, s = qargs[0].shape[:2]
    nb = s // Q_BLOCK
    def blk(a):
        return jnp.swapaxes(a.reshape((b, nb, Q_BLOCK) + a.shape[2:]), 0, 1)
    out = lax.map(lambda xs: fn(*xs[0], xs[1]),
                  (tuple(blk(a) for a in qargs), qpos.reshape(nb, Q_BLOCK)))
    return jnp.swapaxes(out, 0, 1).reshape((b, s) + out.shape[3:])


def _project(xn, lw, pos):
    b, t = xn.shape[:2]
    h = jnp.einsum('btd,dn->btn', xn, lw['w_in'])
    fq, fk, fv, dq, dk, dv, sq, sk, sv, mq, fg = jnp.split(h, SPLIT_POINTS, axis=-1)
    heads = lambda a: a.reshape(b, t, -1, HEAD_DIM)
    logf = jax.nn.log_sigmoid((fg + lw['b_fox_f']).astype(jnp.float32))
    dq = _partial_rope(dq.reshape(b, t, H_DIFF, 2, DIFF_DC), pos)
    dk = _partial_rope(dk.reshape(b, t, H_DIFF, 2, DIFF_DC), pos)
    return (heads(fq), heads(fk), heads(fv), logf, dq, dk, heads(dv),
            heads(sq), heads(sk), heads(sv), heads(mq))


def _diff_lambda(lw, lam_init):
    e = lambda a, c: jnp.exp(jnp.sum(a.astype(jnp.float32) * c.astype(jnp.float32)))
    return e(lw['lq1'], lw['lk1']) - e(lw['lq2'], lw['lk2']) + lam_init


def _merge(xn, lw, fox_o, diff_o, sb_o, mem_o, lam_init):
    b, t = xn.shape[:2]
    diff_o = _rms_norm(diff_o, lw['g_diff']) * (1.0 - lam_init)
    o = jnp.stack([a.reshape(b, t, BRANCH_W) for a in (fox_o, diff_o, sb_o, mem_o)], axis=2)
    proj = jnp.einsum('btnc,ncd->btnd', o, lw['w_branch'])
    gate = jax.nn.sigmoid(jnp.einsum('btd,de->bte', xn, lw['w_gate']) + lw['b_gate'])
    gate = gate.reshape(b, t, N_BRANCH, D_MODEL)
    return jnp.einsum('btd,de->bte', jnp.sum(gate * proj, axis=2), lw['w_out'])


def _conv_ffn(xn, lw, prev):
    h = jnp.einsum('btd,df->btf', xn, lw['w_ffn_in'])
    a, u = jnp.split(h, 2, axis=-1)
    ext = jnp.concatenate([prev.astype(a.dtype), a], axis=1)
    c = lax.conv_general_dilated(ext, lw['conv_w'][:, None, :], (1,), 'VALID',
                                 dimension_numbers=('NWC', 'WIO', 'NWC'),
                                 feature_group_count=D_FF) + lw['conv_b']
    y = jax.nn.gelu(c, approximate=True) * u
    return jnp.einsum('btf,fd->btd', y, lw['w_ffn_out']), ext[:, -(CONV_W - 1):]


def _prompt_layer(x, mem, lw, lam_init):
    b, s = x.shape[:2]
    pos = jnp.arange(s)
    xn = _rms_norm(x, lw['g_mix_pre'])
    fq, fk, fv, logf, dq, dk, dv, sq, sk, sv, mq = _project(xn, lw, pos)
    lam = _diff_lambda(lw, lam_init)
    e = lax.cumsum(logf, axis=1, reverse=True) - logf
    fox_o = _sweep_blocks(lambda q, eq, qp: _fox_attend(q, eq, qp, fk, fv, e, pos), (fq, e), pos)
    diff_o = _sweep_blocks(lambda q, qp: _diff_attend(q, qp, dk, dv, pos, lam), (dq,), pos)
    sb_o = _sweep_blocks(lambda q, qp: _sb_attend(q, qp, sk, sv, pos), (sq,), pos)
    mn = _rms_norm(mem, lw['g_mem'])
    n_mem = mem.shape[1]
    mk = jnp.einsum('bmd,dc->bmc', mn, lw['w_mem_k']).reshape(b, n_mem, H_MEM, HEAD_DIM)
    mv = jnp.einsum('bmd,dc->bmc', mn, lw['w_mem_v']).reshape(b, n_mem, H_MEM, HEAD_DIM)
    mem_o = _mem_attend(mq, mk, mv)
    h = x + _rms_norm(_merge(xn, lw, fox_o, diff_o, sb_o, mem_o, lam_init), lw['g_mix_post'])
    f, conv_state = _conv_ffn(_rms_norm(h, lw['g_ffn_pre']), lw,
                              jnp.zeros((b, CONV_W - 1, D_FF), x.dtype))
    y = h + _rms_norm(f, lw['g_ffn_post'])
    rows = (fk, fv, logf, dk.reshape(b, s, H_DIFF, 2 * DIFF_DC), dv, sk, sv, mk, mv, conv_state)
    return y, rows


def _sample_layer(x, lw, lam_init, page_table, c_fk, c_fv, c_fl, c_dk, c_dv, c_sk, c_sv,
                  m_k, m_v, conv_prev):
    b, t = x.shape[:2]
    past_len = page_table.shape[1] * PAGE_SIZE
    pos = past_len + jnp.arange(t)
    kpos = jnp.arange(past_len + t)
    gather = lambda c: c[page_table].reshape((b, past_len) + c.shape[2:])
    cat = lambda c, new: jnp.concatenate([gather(c).astype(new.dtype), new], axis=1)
    xn = _rms_norm(x, lw['g_mix_pre'])
    fq, fk, fv, logf, dq, dk, dv, sq, sk, sv, mq = _project(xn, lw, pos)
    lam = _diff_lambda(lw, lam_init)
    fl_all = cat(c_fl, logf)
    e_all = lax.cumsum(fl_all, axis=1, reverse=True) - fl_all
    fox_o = _fox_attend(fq, e_all[:, past_len:], pos, cat(c_fk, fk), cat(c_fv, fv), e_all, kpos)
    dk_rows = dk.reshape(b, t, H_DIFF, 2 * DIFF_DC)
    dk_all = cat(c_dk, dk_rows).reshape(b, past_len + t, H_DIFF, 2, DIFF_DC)
    diff_o = _diff_attend(dq, pos, dk_all, cat(c_dv, dv), kpos, lam)
    sb_o = _sb_attend(sq, pos, cat(c_sk, sk), cat(c_sv, sv), kpos)
    mem_o = _mem_attend(mq, m_k.astype(mq.dtype), m_v.astype(mq.dtype))
    h = x + _rms_norm(_merge(xn, lw, fox_o, diff_o, sb_o, mem_o, lam_init), lw['g_mix_post'])
    f, conv_state = _conv_ffn(_rms_norm(h, lw['g_ffn_pre']), lw, conv_prev)
    y = h + _rms_norm(f, lw['g_ffn_post'])
    return y, (fk, fv, logf, dk_rows, dv, sk, sv, conv_state)


def setup_inputs(seed: int = 0) -> dict:
    key = jax.random.key(seed)
    ks = iter(jax.random.split(key, 48))
    nrm = lambda shape, scale=1.0: jax.random.normal(next(ks), shape, jnp.float32) * scale
    gain = lambda shape: 1.0 + nrm(shape, 0.02)
    n_pages = PAST_LEN // PAGE_SIZE
    n_used = DEC_BATCH * n_pages
    n_pool = n_used + (n_used + 3) // 4
    kv_shape = (DEPTH, n_pool, PAGE_SIZE, 4, HEAD_DIM)
    inp = {}
    inp['x_prompt'] = nrm((BATCH, SEQ, D_MODEL))
    inp['mem_prompt'] = nrm((BATCH, N_MEM, D_MODEL))
    inp['x_sample'] = nrm((DEC_BATCH, DEC_SEQ, D_MODEL))
    inp['cache_fox_k'] = nrm(kv_shape)
    inp['cache_fox_v'] = nrm(kv_shape)
    inp['cache_fox_logf'] = jax.nn.log_sigmoid(nrm((DEPTH, n_pool, PAGE_SIZE, H_FOX)) + 2.5)
    inp['cache_diff_k'] = nrm(kv_shape)
    inp['cache_diff_v'] = nrm(kv_shape)
    inp['cache_sb_k'] = nrm(kv_shape)
    inp['cache_sb_v'] = nrm(kv_shape)
    inp['cache_mem_k'] = nrm((DEPTH, DEC_BATCH, N_MEM, H_MEM, HEAD_DIM))
    inp['cache_mem_v'] = nrm((DEPTH, DEC_BATCH, N_MEM, H_MEM, HEAD_DIM))
    inp['state_conv'] = nrm((DEPTH, DEC_BATCH, CONV_W - 1, D_FF))
    perm = jax.random.permutation(next(ks), n_pool)[:n_used]
    inp['page_table'] = perm.reshape(DEC_BATCH, n_pages).astype(jnp.int32)
    inp['g_mix_pre'] = gain((DEPTH, D_MODEL))
    inp['g_mix_post'] = gain((DEPTH, D_MODEL))
    inp['g_ffn_pre'] = gain((DEPTH, D_MODEL))
    inp['g_ffn_post'] = gain((DEPTH, D_MODEL))
    inp['g_mem'] = gain((DEPTH, D_MODEL))
    inp['w_in'] = nrm((DEPTH, D_MODEL, N_IN), D_MODEL ** -0.5)
    inp['b_fox_f'] = jax.random.uniform(next(ks), (DEPTH, H_FOX), jnp.float32, 1.0, 4.0)
    inp['diff_lq1'] = nrm((DEPTH, DIFF_DC), 0.1)
    inp['diff_lk1'] = nrm((DEPTH, DIFF_DC), 0.1)
    inp['diff_lq2'] = nrm((DEPTH, DIFF_DC), 0.1)
    inp['diff_lk2'] = nrm((DEPTH, DIFF_DC), 0.1)
    inp['g_diff'] = gain((DEPTH, 2 * DIFF_DC))
    inp['w_mem_k'] = nrm((DEPTH, D_MODEL, H_MEM * HEAD_DIM), D_MODEL ** -0.5)
    inp['w_mem_v'] = nrm((DEPTH, D_MODEL, H_MEM * HEAD_DIM), D_MODEL ** -0.5)
    inp['w_branch'] = nrm((DEPTH, N_BRANCH, BRANCH_W, D_MODEL), BRANCH_W ** -0.5)
    inp['w_gate'] = nrm((DEPTH, D_MODEL, N_BRANCH * D_MODEL), D_MODEL ** -0.5)
    inp['b_gate'] = nrm((DEPTH, N_BRANCH * D_MODEL), 0.1)
    inp['w_out'] = nrm((DEPTH, D_MODEL, D_MODEL), D_MODEL ** -0.5)
    inp['w_ffn_in'] = nrm((DEPTH, D_MODEL, 2 * D_FF), D_MODEL ** -0.5)
    inp['conv_w'] = nrm((DEPTH, CONV_W, D_FF), CONV_W ** -0.5)
    inp['conv_b'] = nrm((DEPTH, D_FF), 0.02)
    inp['w_ffn_out'] = nrm((DEPTH, D_FF, D_MODEL), D_FF ** -0.5)
    return inp


def reference(x_prompt, mem_prompt, x_sample, cache_fox_k, cache_fox_v, cache_fox_logf,
              cache_diff_k, cache_diff_v, cache_sb_k, cache_sb_v, cache_mem_k, cache_mem_v,
              state_conv, page_table, g_mix_pre, g_mix_post, g_ffn_pre, g_ffn_post, g_mem,
              w_in, b_fox_f, diff_lq1, diff_lk1, diff_lq2, diff_lk2, g_diff, w_mem_k, w_mem_v,
              w_branch, w_gate, b_gate, w_out, w_ffn_in, conv_w, conv_b, w_ffn_out):
    xp = x_prompt
    xs = x_sample
    p_rows = []
    s_rows = []
    for l in range(DEPTH):
        lam_init = 0.8 - 0.6 * math.exp(-0.3 * l)
        lw = dict(g_mix_pre=g_mix_pre[l], g_mix_post=g_mix_post[l], g_ffn_pre=g_ffn_pre[l],
                  g_ffn_post=g_ffn_post[l], g_mem=g_mem[l], w_in=w_in[l], b_fox_f=b_fox_f[l],
                  lq1=diff_lq1[l], lk1=diff_lk1[l], lq2=diff_lq2[l], lk2=diff_lk2[l],
                  g_diff=g_diff[l], w_mem_k=w_mem_k[l], w_mem_v=w_mem_v[l],
                  w_branch=w_branch[l], w_gate=w_gate[l], b_gate=b_gate[l], w_out=w_out[l],
                  w_ffn_in=w_ffn_in[l], conv_w=conv_w[l], conv_b=conv_b[l],
                  w_ffn_out=w_ffn_out[l])
        xp, pr = _prompt_layer(xp, mem_prompt, lw, lam_init)
        p_rows.append(pr)
        xs, sr = _sample_layer(xs, lw, lam_init, page_table, cache_fox_k[l], cache_fox_v[l],
                               cache_fox_logf[l], cache_diff_k[l], cache_diff_v[l],
                               cache_sb_k[l], cache_sb_v[l], cache_mem_k[l], cache_mem_v[l],
                               state_conv[l])
        s_rows.append(sr)
    (p_fk, p_fv, p_fl, p_dk, p_dv, p_sk, p_sv, p_mk, p_mv, p_cv) = [jnp.stack(a) for a in zip(*p_rows)]
    (s_fk, s_fv, s_fl, s_dk, s_dv, s_sk, s_sv, s_cv) = [jnp.stack(a) for a in zip(*s_rows)]
    return (xp, xs, p_fk, p_fv, p_fl, p_dk, p_dv, p_sk, p_sv, p_mk, p_mv, p_cv,
            s_fk, s_fv, s_fl, s_dk, s_dv, s_sk, s_sv, s_cv)
```

```python
import functools
import math

import jax
import jax.numpy as jnp
from jax import lax
from jax.experimental import pallas as pl
from jax.experimental.pallas import tpu as pltpu

F32 = jnp.float32
BF16 = jnp.bfloat16

HEAD_DIM = 64
N_HEADS = 4
BRANCH_W = N_HEADS * HEAD_DIM
DIFF_DC = HEAD_DIM // 2
N_BRANCH = 4
PAGE_SIZE = 128
ROPE_THETA = 500000.0
ROPE_ROT = DIFF_DC // 4
RMS_EPS = 1e-6
ATTN_SCALE = HEAD_DIM ** -0.5
DIFF_SCALE = DIFF_DC ** -0.5
NEG = -1e30

LANES = 128
SUBLANES = 8
VMEM_LIMIT_BYTES = 56 * 1024 * 1024
ATT_BLOCK = 256
FF_CHUNK = 256
PAGES_PER_STEP = 8


def _call(kernel, *, grid, in_specs, out_specs, out_shape, scratch=(), prefetch=0, name):
    spec = pltpu.PrefetchScalarGridSpec(num_scalar_prefetch=prefetch, grid=grid, in_specs=in_specs,
                                        out_specs=out_specs, scratch_shapes=list(scratch))
    params = pltpu.CompilerParams(dimension_semantics=("arbitrary",) * len(grid),
                                  vmem_limit_bytes=VMEM_LIMIT_BYTES)
    return pl.pallas_call(kernel, grid_spec=spec, out_shape=out_shape, compiler_params=params, name=name)


def _resident(shape):
    zeros = (0,) * len(shape)
    return pl.BlockSpec(shape, lambda *_: zeros, pipeline_mode=pl.Buffered(1))


def _rms(x, g):
    return x * lax.rsqrt(jnp.mean(x * x, axis=-1, keepdims=True) + RMS_EPS) * g


def _log_sigmoid(z):
    return jnp.minimum(z, 0.0) - jnp.log1p(jnp.exp(-jnp.abs(z)))


def _dot(a, b):
    return jnp.dot(a, b, preferred_element_type=F32)


def _dot_nt(a, b):
    return lax.dot_general(a, b, (((1,), (1,)), ((), ())), preferred_element_type=F32)


def _rope_rows(h, c, s1, s2):
    return h * c + pltpu.roll(h, BRANCH_W - ROPE_ROT // 2, 1) * s1 + pltpu.roll(h, ROPE_ROT // 2, 1) * s2


def _rope_cols(h, c, s1, s2):
    return h * c + pltpu.roll(h, BRANCH_W - ROPE_ROT // 2, 0) * s1 + pltpu.roll(h, ROPE_ROT // 2, 0) * s2


def _proj_prompt_kernel(x_ref, g_ref, wq_ref, wkvt_ref, wft_ref, bf_ref, rc_ref, rs1_ref, rs2_ref,
                        rct_ref, rs1t_ref, rs2t_ref,
                        fq_ref, dq_ref, sq_ref, mq_ref, fk_ref, fv_ref, dk_ref, dv_ref, sk_ref, sv_ref, lf_ref):
    xn = _rms(x_ref[...], g_ref[...]).astype(BF16)
    for gi, out in enumerate((fq_ref, dq_ref, sq_ref, mq_ref)):
        h = _dot(xn, wq_ref[:, gi * BRANCH_W:(gi + 1) * BRANCH_W])
        if out is dq_ref:
            h = _rope_rows(h, rc_ref[...], rs1_ref[...], rs2_ref[...])
        out[...] = h
    for gi, out in enumerate((fk_ref, fv_ref, dk_ref, dv_ref, sk_ref, sv_ref)):
        ht = _dot_nt(wkvt_ref[gi * BRANCH_W:(gi + 1) * BRANCH_W, :], xn)
        if out is dk_ref:
            ht = _rope_cols(ht, rct_ref[...], rs1t_ref[...], rs2t_ref[...])
        out[0] = ht
    lf_ref[0] = _log_sigmoid(_dot_nt(wft_ref[...], xn) + bf_ref[...])


def _proj_decode_kernel(x_ref, g_ref, w_ref, wft_ref, bf_ref, rc_ref, rs1_ref, rs2_ref, *outs):
    xn = _rms(x_ref[...], g_ref[...]).astype(BF16)
    for gi in range(10):
        h = _dot(xn, w_ref[:, gi * BRANCH_W:(gi + 1) * BRANCH_W])
        if gi in (1, 6):
            h = _rope_rows(h, rc_ref[...], rs1_ref[...], rs2_ref[...])
        outs[gi][...] = h
    outs[10][...] = _log_sigmoid(_dot_nt(wft_ref[...], xn) + bf_ref[...])


def _rope_tables(pos):
    half = ROPE_ROT // 2
    inv_freq = ROPE_THETA ** (-jnp.arange(half, dtype=F32) * 2.0 / ROPE_ROT)
    ang = pos.astype(F32)[:, None] * inv_freq[None, :]
    cos, sin = jnp.cos(ang), jnp.sin(ang)
    n = pos.shape[0]
    pad = jnp.zeros((n, DIFF_DC - ROPE_ROT), F32)
    c = jnp.concatenate([cos, cos, pad + 1.0], axis=1)
    s1 = jnp.concatenate([-sin, jnp.zeros_like(sin), pad], axis=1)
    s2 = jnp.concatenate([jnp.zeros_like(sin), sin, pad], axis=1)
    reps = BRANCH_W // DIFF_DC
    return tuple(jnp.tile(t, (1, reps)) for t in (c, s1, s2))


def _project_prompt(x2, batch, seq, p):
    t = batch * seq
    tm = 512 if seq % 512 == 0 else seq
    per_seq = seq // tm
    rc, rs1, rs2 = _rope_tables(jnp.arange(seq))
    rct, rs1t, rs2t = rc.T, rs1.T, rs2.T
    d = x2.shape[1]
    row = lambda i: (i, 0)
    tab = lambda i: (i % per_seq, 0)
    tabt = lambda i: (0, i % per_seq)
    kv = lambda i: (i // per_seq, 0, i % per_seq)
    in_specs = [pl.BlockSpec((tm, d), row), _resident((1, d)), _resident(p['wq'].shape), _resident(p['wkvt'].shape),
                _resident(p['wft'].shape), _resident((SUBLANES, 1)),
                pl.BlockSpec((tm, BRANCH_W), tab), pl.BlockSpec((tm, BRANCH_W), tab), pl.BlockSpec((tm, BRANCH_W), tab),
                pl.BlockSpec((BRANCH_W, tm), tabt), pl.BlockSpec((BRANCH_W, tm), tabt), pl.BlockSpec((BRANCH_W, tm), tabt)]
    out_specs = ([pl.BlockSpec((tm, BRANCH_W), row)] * 4 + [pl.BlockSpec((1, BRANCH_W, tm), kv)] * 6
                 + [pl.BlockSpec((1, SUBLANES, tm), kv)])
    out_shape = ([jax.ShapeDtypeStruct((t, BRANCH_W), F32)] * 4
                 + [jax.ShapeDtypeStruct((batch, BRANCH_W, seq), F32)] * 6
                 + [jax.ShapeDtypeStruct((batch, SUBLANES, seq), F32)])
    return _call(_proj_prompt_kernel, grid=(t // tm,), in_specs=in_specs, out_specs=out_specs, out_shape=out_shape,
                 name="proj_prompt")(x2, p['g_mix_pre'], p['wq'], p['wkvt'], p['wft'], p['bf'],
                                     rc, rs1, rs2, rct, rs1t, rs2t)


def _project_decode(x2, pos, p):
    m, d = x2.shape
    rc, rs1, rs2 = _rope_tables(jnp.full((m,), pos))
    full = lambda shape: pl.BlockSpec(shape, lambda i: (0,) * len(shape))
    in_specs = [full((m, d)), full((1, d)), full(p['w_all'].shape), full(p['wft'].shape), full((SUBLANES, 1)),
                full((m, BRANCH_W)), full((m, BRANCH_W)), full((m, BRANCH_W))]
    out_specs = [full((m, BRANCH_W))] * 10 + [full((SUBLANES, m))]
    out_shape = [jax.ShapeDtypeStruct((m, BRANCH_W), F32)] * 10 + [jax.ShapeDtypeStruct((SUBLANES, m), F32)]
    return _call(_proj_decode_kernel, grid=(1,), in_specs=in_specs, out_specs=out_specs, out_shape=out_shape,
                 name="proj_decode")(x2, p['g_mix_pre'], p['w_all'], p['wft'], p['bf'], rc, rs1, rs2)


def _suffix_sum_lanes(x):
    n = x.shape[1]
    lane = lax.broadcasted_iota(jnp.int32, x.shape, 1)
    d = 1
    while d < n:
        x = x + jnp.where(lane + d < n, pltpu.roll(x, n - d, 1), 0.0)
        d *= 2
    return x


def _suffix_kernel(lf_ref, e_ref):
    lf = lf_ref[0]
    e_ref[0] = _suffix_sum_lanes(lf) - lf


def _fox_suffix(lft):
    b, r, s = lft.shape
    spec = pl.BlockSpec((1, r, s), lambda i: (i, 0, 0))
    return _call(_suffix_kernel, grid=(b,), in_specs=[spec], out_specs=spec,
                 out_shape=jax.ShapeDtypeStruct(lft.shape, F32), name="fox_suffix")(lft)


def _group_mask(shape, axis, width, g):
    idx = lax.broadcasted_iota(jnp.int32, shape, axis)
    return (idx >= g * width) & (idx < (g + 1) * width)


def _build_q_stack(q_ref, qs_ref, n_groups, scale, tq):
    nq = qs_ref.shape[0]
    width = BRANCH_W // n_groups
    for i in range(nq):
        q = q_ref[0, i * tq:(i + 1) * tq, :] * scale
        for g in range(n_groups):
            qs_ref[i, g * tq:(g + 1) * tq, :] = jnp.where(_group_mask(q.shape, 1, width, g), q, 0.0).astype(BF16)


def _value_block_diag(vt):
    return jnp.concatenate([jnp.where(_group_mask(vt.shape, 0, HEAD_DIM, h), vt, jnp.zeros_like(vt))
                            for h in range(N_HEADS)], axis=1)


def _per_head_lanes(cols):
    tq = cols[0].shape[0]
    lane = lax.broadcasted_iota(jnp.int32, (tq, LANES), 1)
    lo = jnp.where(lane < HEAD_DIM, cols[0], cols[1])
    hi = jnp.where(lane < HEAD_DIM, cols[2], cols[3])
    return jnp.concatenate([lo, hi], axis=1)


def _softmax_update(u, m_ref, l_ref, rows, col, shift=None):
    m_old = m_ref[rows, col:col + 1]
    mu = jnp.max(u, axis=1, keepdims=True)
    if shift is not None:
        mu = mu - shift
    m_new = jnp.maximum(m_old, mu)
    ref = m_new if shift is None else m_new + shift
    p = jnp.exp(u - ref)
    alpha = jnp.exp(m_old - m_new)
    l_ref[rows, col:col + 1] = alpha * l_ref[rows, col:col + 1] + jnp.sum(p, axis=1, keepdims=True)
    m_ref[rows, col:col + 1] = m_new
    return p.astype(BF16), alpha


def _fox_prompt_kernel(q_ref, kt_ref, vt_ref, erow_ref, ecol_ref, o_ref, qs_ref, acc_ref, m_ref, l_ref):
    tq = tk = ATT_BLOCK
    nq = qs_ref.shape[0]
    j = pl.program_id(1)

    @pl.when(j == 0)
    def _():
        _build_q_stack(q_ref, qs_ref, N_HEADS, ATTN_SCALE, tq)
        m_ref[...] = jnp.full(m_ref.shape, NEG, F32)
        l_ref[...] = jnp.zeros(l_ref.shape, F32)
        acc_ref[...] = jnp.zeros(acc_ref.shape, F32)

    kt = kt_ref[0].astype(BF16)
    vbd = _value_block_diag(vt_ref[0].astype(BF16))
    erow = erow_ref[0]
    causal = lax.broadcasted_iota(jnp.int32, (tq, tk), 1) <= lax.broadcasted_iota(jnp.int32, (tq, tk), 0)

    def q_block(i, masked):
        rows = pl.ds(pl.multiple_of(i * tq, tq), tq)
        s_all = _dot(qs_ref[i], kt)
        ps, alphas = [], []
        for h in range(N_HEADS):
            u = s_all[h * tq:(h + 1) * tq] + erow[h:h + 1, :]
            if masked:
                u = jnp.where(causal, u, NEG)
            p, alpha = _softmax_update(u, m_ref, l_ref, rows, h, shift=ecol_ref[0, rows, h:h + 1])
            ps.append(p)
            alphas.append(alpha)
        pv = _dot_nt(jnp.concatenate(ps, axis=1), vbd)
        acc_ref[rows, :] = acc_ref[rows, :] * _per_head_lanes(alphas) + pv

    q_block(j, True)
    lax.fori_loop(j + 1, nq, lambda i, c: (q_block(i, False), c)[1], 0)

    @pl.when(j == pl.num_programs(1) - 1)
    def _():
        for i in range(nq):
            rows = slice(i * tq, (i + 1) * tq)
            inv = _per_head_lanes([1.0 / l_ref[rows, h:h + 1] for h in range(N_HEADS)])
            o_ref[0, rows, :] = acc_ref[rows, :] * inv


def _diff_lambda(lq1_ref, lk1_ref, lq2_ref, lk2_ref, lam_init):
    e1 = jnp.exp(jnp.sum(lq1_ref[...] * lk1_ref[...], axis=1, keepdims=True))
    e2 = jnp.exp(jnp.sum(lq2_ref[...] * lk2_ref[...], axis=1, keepdims=True))
    return e1 - e2 + lam_init


def _diff_prompt_kernel(lam_init, q_ref, kt_ref, vt_ref, lq1_ref, lk1_ref, lq2_ref, lk2_ref, o_ref,
                        qs_ref, acc1_ref, acc2_ref, m_ref, l_ref):
    tq = tk = ATT_BLOCK
    nq = qs_ref.shape[0]
    j = pl.program_id(1)

    @pl.when(j == 0)
    def _():
        _build_q_stack(q_ref, qs_ref, 2 * N_HEADS, DIFF_SCALE, tq)
        m_ref[...] = jnp.full(m_ref.shape, NEG, F32)
        l_ref[...] = jnp.zeros(l_ref.shape, F32)
        acc1_ref[...] = jnp.zeros(acc1_ref.shape, F32)
        acc2_ref[...] = jnp.zeros(acc2_ref.shape, F32)

    kt = kt_ref[0].astype(BF16)
    vbd = _value_block_diag(vt_ref[0].astype(BF16))
    causal = lax.broadcasted_iota(jnp.int32, (tq, tk), 1) <= lax.broadcasted_iota(jnp.int32, (tq, tk), 0)

    def q_block(i, masked):
        rows = pl.ds(pl.multiple_of(i * tq, tq), tq)
        s_all = _dot(qs_ref[i], kt)
        for comp, acc_ref in ((0, acc1_ref), (1, acc2_ref)):
            ps, alphas = [], []
            for h in range(N_HEADS):
                g = 2 * h + comp
                u = s_all[g * tq:(g + 1) * tq]
                if masked:
                    u = jnp.where(causal, u, NEG)
                p, alpha = _softmax_update(u, m_ref, l_ref, rows, g)
                ps.append(p)
                alphas.append(alpha)
            pv = _dot_nt(jnp.concatenate(ps, axis=1), vbd)
            acc_ref[rows, :] = acc_ref[rows, :] * _per_head_lanes(alphas) + pv

    q_block(j, True)
    lax.fori_loop(j + 1, nq, lambda i, c: (q_block(i, False), c)[1], 0)

    @pl.when(j == pl.num_programs(1) - 1)
    def _():
        lam = _diff_lambda(lq1_ref, lk1_ref, lq2_ref, lk2_ref, lam_init)
        for i in range(nq):
            rows = slice(i * tq, (i + 1) * tq)
            inv1 = _per_head_lanes([1.0 / l_ref[rows, 2 * h:2 * h + 1] for h in range(N_HEADS)])
            inv2 = _per_head_lanes([1.0 / l_ref[rows, 2 * h + 1:2 * h + 2] for h in range(N_HEADS)])
            o_ref[0, rows, :] = acc1_ref[rows, :] * inv1 - lam * (acc2_ref[rows, :] * inv2)


def _split_bf16(x):
    hi = x.astype(BF16)
    lo = (x - hi.astype(F32)).astype(BF16)
    return hi, lo


def _sb_prompt_kernel(q_ref, kt_ref, vt_ref, later_ref, o_ref, qs_ref, acc_ref, r_ref):
    tq = tk = ATT_BLOCK
    nq = qs_ref.shape[0]
    j = pl.program_id(1)
    jb = pl.num_programs(1) - 1 - j

    @pl.when(j == 0)
    def _():
        _build_q_stack(q_ref, qs_ref, N_HEADS, ATTN_SCALE, tq)
        r_ref[...] = jnp.zeros(r_ref.shape, F32)
        acc_ref[...] = jnp.zeros(acc_ref.shape, F32)

    kt = kt_ref[0].astype(BF16)
    vbd = _value_block_diag(vt_ref[0].astype(BF16))
    later = later_ref[...]
    strict = lax.broadcasted_iota(jnp.int32, (tq, tk), 1) < lax.broadcasted_iota(jnp.int32, (tq, tk), 0)

    def q_block(i, masked):
        rows = pl.ds(pl.multiple_of(i * tq, tq), tq)
        z_all = _dot(qs_ref[i], kt)
        log_beta, his, los = [], [], []
        for h in range(N_HEADS):
            z = z_all[h * tq:(h + 1) * tq]
            ls = _log_sigmoid(z)
            keep = ls - z
            if masked:
                keep = jnp.where(strict, keep, 0.0)
            hi, lo = _split_bf16(keep)
            log_beta.append(ls)
            his.append(hi)
            los.append(lo)
            r_old = r_ref[rows, h:h + 1]
            r_ref[rows, h:h + 1] = r_old + jnp.sum(keep, axis=1, keepdims=True)
            log_beta[h] = ls + r_old
        tails = _dot(jnp.concatenate(his + los, axis=0), later)
        ps = []
        for h in range(N_HEADS):
            tail = tails[h * tq:(h + 1) * tq] + tails[(N_HEADS + h) * tq:(N_HEADS + h + 1) * tq]
            a = jnp.exp(log_beta[h] + tail)
            if masked:
                a = jnp.where(strict, a, 0.0)
            ps.append(a.astype(BF16))
        acc_ref[rows, :] += _dot_nt(jnp.concatenate(ps, axis=1), vbd)

    q_block(jb, True)
    lax.fori_loop(jb + 1, nq, lambda i, c: (q_block(i, False), c)[1], 0)

    @pl.when(j == pl.num_programs(1) - 1)
    def _():
        o_ref[0] = acc_ref[...]


def _prompt_attention(kind, q, kt, vt, extra, lam_init=None):
    b, s, _ = q.shape
    tq = tk = ATT_BLOCK
    nq, nk = s // tq, s // tk
    kidx = (lambda bi, j: (bi, 0, nk - 1 - j)) if kind == "sb" else (lambda bi, j: (bi, 0, j))
    whole = pl.BlockSpec((1, s, BRANCH_W), lambda bi, j: (bi, 0, 0))
    kv_spec = pl.BlockSpec((1, BRANCH_W, tk), kidx)
    in_specs = [whole, kv_spec, kv_spec]
    state = pltpu.VMEM((s, LANES), F32)
    acc = pltpu.VMEM((s, BRANCH_W), F32)
    if kind == "fox":
        erow, ecol = extra
        in_specs += [pl.BlockSpec((1, SUBLANES, tk), kidx), pl.BlockSpec((1, s, N_HEADS), lambda bi, j: (bi, 0, 0))]
        kernel, groups, scratch = _fox_prompt_kernel, N_HEADS, [acc, state, state]
    elif kind == "diff":
        in_specs += [_resident((1, DIFF_DC))] * 4
        kernel, groups, scratch = functools.partial(_diff_prompt_kernel, lam_init), 2 * N_HEADS, [acc, acc, state, state]
    else:
        in_specs += [_resident((tk, tk))]
        kernel, groups, scratch = _sb_prompt_kernel, N_HEADS, [acc, state]
    scratch = [pltpu.VMEM((nq, groups * tq, BRANCH_W), BF16)] + scratch
    return _call(kernel, grid=(b, nk), in_specs=in_specs, out_specs=whole,
                 out_shape=jax.ShapeDtypeStruct(q.shape, F32), scratch=scratch, name=kind + "_prompt")(q, kt, vt, *extra)


def _mem_kv_kernel(mem_ref, g_ref, wkt_ref, wvt_ref, mk_ref, mv_ref):
    mn = _rms(mem_ref[0], g_ref[...]).astype(BF16)
    mk_ref[0] = _dot_nt(wkt_ref[...], mn)
    mv_ref[0] = _dot_nt(wvt_ref[...], mn)


def _mem_kv(mem, p):
    b, n, d = mem.shape
    out = pl.BlockSpec((1, BRANCH_W, n), lambda i: (i, 0, 0))
    return _call(_mem_kv_kernel, grid=(b,),
                 in_specs=[pl.BlockSpec((1, n, d), lambda i: (i, 0, 0)), _resident((1, d)),
                           _resident((BRANCH_W, d)), _resident((BRANCH_W, d))],
                 out_specs=[out, out], out_shape=[jax.ShapeDtypeStruct((b, BRANCH_W, n), F32)] * 2,
                 name="mem_kv")(mem, p['g_mem'], p['wmkt'], p['wmvt'])


def _mem_attn_kernel(q_ref, kt_ref, vt_ref, o_ref):
    tq = q_ref.shape[1]
    q = q_ref[0] * ATTN_SCALE
    qs = jnp.concatenate([jnp.where(_group_mask(q.shape, 1, HEAD_DIM, h), q, 0.0) for h in range(N_HEADS)], axis=0)
    s_all = _dot(qs.astype(BF16), kt_ref[0].astype(BF16))
    vbd = _value_block_diag(vt_ref[0].astype(BF16))
    ps, inv = [], []
    for h in range(N_HEADS):
        s = s_all[h * tq:(h + 1) * tq]
        p = jnp.exp(s - jnp.max(s, axis=1, keepdims=True))
        inv.append(1.0 / jnp.sum(p, axis=1, keepdims=True))
        ps.append(p.astype(BF16))
    o_ref[0] = _dot_nt(jnp.concatenate(ps, axis=1), vbd) * _per_head_lanes(inv)


def _mem_attention(q, kt, vt, tq, kv_base):
    b, s, _ = q.shape
    n = kt.shape[2]
    qspec = pl.BlockSpec((1, tq, BRANCH_W), lambda bi, i: (bi, i, 0))
    kvspec = pl.BlockSpec((1, BRANCH_W, n), lambda bi, i: (kv_base + bi, 0, 0))
    return _call(_mem_attn_kernel, grid=(b, s // tq), in_specs=[qspec, kvspec, kvspec], out_specs=qspec,
                 out_shape=jax.ShapeDtypeStruct(q.shape, F32), name="mem_attn")(q, kt, vt)


def _merge_kernel(lam_init, x_ref, fo_ref, do_ref, so_ref, mo_ref, gpre_ref, gdiff_ref, hsum_ref,
                  wb_ref, wg_ref, bg_ref, wo_ref, gpost_ref, out_ref):
    x = x_ref[...]
    d_model = x.shape[1]
    xn = _rms(x, gpre_ref[...]).astype(BF16)
    d = do_ref[...]
    hi, lo = _split_bf16(d * d)
    ms = (_dot(hi, hsum_ref[...]) + _dot(lo, hsum_ref[...])) * (1.0 / HEAD_DIM)
    dn = d * lax.rsqrt(ms + RMS_EPS) * gdiff_ref[...] * (1.0 - lam_init)
    acc = jnp.zeros((x.shape[0], d_model), F32)
    for n, o in enumerate((fo_ref[...], dn, so_ref[...], mo_ref[...])):
        proj = _dot(o.astype(BF16), wb_ref[n])
        gate = jax.nn.sigmoid(_dot(xn, wg_ref[:, n * d_model:(n + 1) * d_model]) + bg_ref[:, n * d_model:(n + 1) * d_model])
        acc = acc + gate * proj
    out_ref[...] = x + _rms(_dot(acc.astype(BF16), wo_ref[...]), gpost_ref[...])


def _merge(x2, fo, do, so, mo, p, lam_init):
    m, d = x2.shape
    tm = 256 if m % 256 == 0 else m
    row = lambda w: pl.BlockSpec((tm, w), lambda i: (i, 0))
    in_specs = [row(d)] + [row(BRANCH_W)] * 4 + [_resident((1, d)), _resident((1, BRANCH_W)), _resident((BRANCH_W, BRANCH_W)),
                                                  _resident(p['wb'].shape), _resident(p['wg'].shape), _resident(p['bg'].shape),
                                                  _resident(p['wo'].shape), _resident((1, d))]
    return _call(functools.partial(_merge_kernel, lam_init), grid=(m // tm,), in_specs=in_specs, out_specs=row(d),
                 out_shape=jax.ShapeDtypeStruct((m, d), F32), name="merge")(
        x2, fo, do, so, mo, p['g_mix_pre'], p['g_diff'], p['hsum'], p['wb'], p['wg'], p['bg'], p['wo'], p['g_mix_post'])


def _ffn_chunk(hn, a1, a2, c, win_ref, cw_ref, cb_ref, wout_ref, d_ff):
    cols = slice(c * FF_CHUNK, (c + 1) * FF_CHUNK)
    a = _dot(hn, win_ref[:, cols])
    u = _dot(hn, win_ref[:, d_ff + c * FF_CHUNK:d_ff + (c + 1) * FF_CHUNK])
    conv = cw_ref[0:1, cols] * a2(a) + cw_ref[1:2, cols] * a1(a) + cw_ref[2:3, cols] * a + cb_ref[:, cols]
    y = jax.nn.gelu(conv, approximate=True) * u
    return a, _dot(y.astype(BF16), wout_ref[cols, :])


def _ffn_prompt_kernel(per_seq, h_ref, gpre_ref, win_ref, cw_ref, cb_ref, wout_ref, gpost_ref, out_ref, conv_ref, carry_ref):
    h = h_ref[...]
    tm = h.shape[0]
    d_ff = cw_ref.shape[1]
    hn = _rms(h, gpre_ref[...]).astype(BF16)

    @pl.when(pl.program_id(0) % per_seq == 0)
    def _():
        carry_ref[...] = jnp.zeros(carry_ref.shape, F32)

    row = lax.broadcasted_iota(jnp.int32, (tm, FF_CHUNK), 0)
    f = jnp.zeros(h.shape, F32)
    for c in range(d_ff // FF_CHUNK):
        cols = slice(c * FF_CHUNK, (c + 1) * FF_CHUNK)
        pm2, pm1 = carry_ref[0:1, cols], carry_ref[1:2, cols]
        a1 = lambda a: jnp.where(row == 0, pm1, pltpu.roll(a, 1, 0))
        a2 = lambda a: jnp.where(row == 0, pm2, jnp.where(row == 1, pm1, pltpu.roll(a, 2, 0)))
        a, part = _ffn_chunk(hn, a1, a2, c, win_ref, cw_ref, cb_ref, wout_ref, d_ff)
        f = f + part
        carry_ref[0:2, cols] = a[tm - 2:tm, :]
        conv_ref[0, :, cols] = a[tm - 2:tm, :]
    out_ref[...] = h + _rms(f, gpost_ref[...])


def _ffn_decode_kernel(h_ref, gpre_ref, win_ref, cw_ref, cb_ref, wout_ref, gpost_ref, p0_ref, p1_ref, out_ref, a_ref):
    h = h_ref[...]
    d_ff = cw_ref.shape[1]
    hn = _rms(h, gpre_ref[...]).astype(BF16)
    f = jnp.zeros(h.shape, F32)
    for c in range(d_ff // FF_CHUNK):
        cols = slice(c * FF_CHUNK, (c + 1) * FF_CHUNK)
        a, part = _ffn_chunk(hn, lambda a: p1_ref[:, cols], lambda a: p0_ref[:, cols], c,
                             win_ref, cw_ref, cb_ref, wout_ref, d_ff)
        f = f + part
        a_ref[:, cols] = a
    out_ref[...] = h + _rms(f, gpost_ref[...])


def _ffn_weights_specs(p, d):
    return [_resident((1, d)), _resident(p['wfi'].shape), _resident(p['cw'].shape), _resident(p['cb'].shape),
            _resident(p['wfo'].shape), _resident((1, d))]


def _ffn_prompt(h2, batch, seq, p):
    m, d = h2.shape
    d_ff = p['cw'].shape[1]
    tm = 256 if seq % 256 == 0 else seq
    per_seq = seq // tm
    row = pl.BlockSpec((tm, d), lambda i: (i, 0))
    return _call(functools.partial(_ffn_prompt_kernel, per_seq), grid=(m // tm,),
                 in_specs=[row] + _ffn_weights_specs(p, d),
                 out_specs=[row, pl.BlockSpec((1, 2, d_ff), lambda i: (i // per_seq, 0, 0))],
                 out_shape=[jax.ShapeDtypeStruct((m, d), F32), jax.ShapeDtypeStruct((batch, 2, d_ff), F32)],
                 scratch=[pltpu.VMEM((SUBLANES, d_ff), F32)], name="ffn_prompt")(
        h2, p['g_ffn_pre'], p['wfi'], p['cw'], p['cb'], p['wfo'], p['g_ffn_post'])


def _ffn_decode(h2, prev, p):
    m, d = h2.shape
    d_ff = p['cw'].shape[1]
    full = lambda shape: pl.BlockSpec(shape, lambda i: (0,) * len(shape))
    y, a = _call(_ffn_decode_kernel, grid=(1,),
                 in_specs=[full((m, d))] + _ffn_weights_specs(p, d) + [full((m, d_ff)), full((m, d_ff))],
                 out_specs=[full((m, d)), full((m, d_ff))],
                 out_shape=[jax.ShapeDtypeStruct((m, d), F32), jax.ShapeDtypeStruct((m, d_ff), F32)],
                 name="ffn_decode")(h2, p['g_ffn_pre'], p['wfi'], p['cw'], p['cb'], p['wfo'], p['g_ffn_post'],
                                    prev[:, 0], prev[:, 1])
    return y, jnp.stack([prev[:, 1], a], axis=1)


def _row_query(q_ref, n_groups, scale):
    shape = (SUBLANES, BRANCH_W)
    width = BRANCH_W // n_groups
    own = lax.broadcasted_iota(jnp.int32, shape, 1) // width == lax.broadcasted_iota(jnp.int32, shape, 0)
    return jnp.where(own, jnp.broadcast_to(q_ref[...] * scale, shape), 0.0), own


def _gather_pages(refs):
    return jnp.concatenate([r[...] for r in refs], axis=1)


def _decode_softmax_step(s, vt, m_ref, l_ref, acc_ref):
    m_old = m_ref[:, 0:1]
    m_new = jnp.maximum(m_old, jnp.max(s, axis=1, keepdims=True))
    p = jnp.exp(s - m_new)
    alpha = jnp.exp(m_old - m_new)
    l_ref[:, 0:1] = alpha * l_ref[:, 0:1] + jnp.sum(p, axis=1, keepdims=True)
    m_ref[:, 0:1] = m_new
    acc_ref[...] = alpha * acc_ref[...] + _dot_nt(p.astype(BF16), vt)


def _decode_self_init(q8, kn_ref, vn_ref, m_ref, l_ref, acc_ref):
    m_ref[:, 0:1] = jnp.sum(q8 * kn_ref[...], axis=1, keepdims=True)
    l_ref[:, 0:1] = jnp.ones((SUBLANES, 1), F32)
    acc_ref[...] = jnp.broadcast_to(vn_ref[...], acc_ref.shape)


def _fox_decode_kernel(pg, pt_ref, q_ref, kn_ref, vn_ref, lfn_ref, *refs):
    lf_refs, k_refs, v_refs = refs[:pg], refs[pg:2 * pg], refs[2 * pg:3 * pg]
    o_ref, m_ref, l_ref, acc_ref, c_ref, lf_ref = refs[3 * pg:]
    j = pl.program_id(1)
    q8, own = _row_query(q_ref, N_HEADS, ATTN_SCALE)

    @pl.when(j == 0)
    def _():
        _decode_self_init(q8, kn_ref, vn_ref, m_ref, l_ref, acc_ref)
        c_ref[...] = jnp.zeros(c_ref.shape, F32)
        c_ref[0:N_HEADS, 0:1] = lfn_ref[...]
        lf_ref[...] = jnp.zeros(lf_ref.shape, F32)

    for t in range(pg):
        lf_ref[0:N_HEADS, t * PAGE_SIZE:(t + 1) * PAGE_SIZE] = lf_refs[t][...]
    lf = lf_ref[...]
    incl = _suffix_sum_lanes(lf)
    carry = c_ref[:, 0:1]
    bias = incl - lf + carry
    c_ref[:, 0:1] = carry + incl[:, 0:1]
    s = _dot(q8.astype(BF16), _gather_pages(k_refs).astype(BF16)) + bias
    _decode_softmax_step(s, _gather_pages(v_refs).astype(BF16), m_ref, l_ref, acc_ref)

    @pl.when(j == pl.num_programs(1) - 1)
    def _():
        o8 = acc_ref[...] / l_ref[:, 0:1]
        o_ref[...] = jnp.sum(jnp.where(own, o8, 0.0), axis=0, keepdims=True)


def _diff_decode_kernel(pg, lam_init, pt_ref, q_ref, kn_ref, vn_ref, lq1_ref, lk1_ref, lq2_ref, lk2_ref, *refs):
    k_refs, v_refs = refs[:pg], refs[pg:2 * pg]
    o_ref, m_ref, l_ref, acc_ref = refs[2 * pg:]
    j = pl.program_id(1)
    q8, _ = _row_query(q_ref, 2 * N_HEADS, DIFF_SCALE)

    @pl.when(j == 0)
    def _():
        _decode_self_init(q8, kn_ref, vn_ref, m_ref, l_ref, acc_ref)

    s = _dot(q8.astype(BF16), _gather_pages(k_refs).astype(BF16))
    _decode_softmax_step(s, _gather_pages(v_refs).astype(BF16), m_ref, l_ref, acc_ref)

    @pl.when(j == pl.num_programs(1) - 1)
    def _():
        lam = _diff_lambda(lq1_ref, lk1_ref, lq2_ref, lk2_ref, lam_init)
        shape = (SUBLANES, BRANCH_W)
        row = lax.broadcasted_iota(jnp.int32, shape, 0)
        head = lax.broadcasted_iota(jnp.int32, shape, 1) // HEAD_DIM
        w = jnp.where(row % 2 == 0, 1.0, -lam)
        o8 = acc_ref[...] / l_ref[:, 0:1] * w
        o_ref[...] = jnp.sum(jnp.where(head == row // 2, o8, 0.0), axis=0, keepdims=True)


def _sb_decode_kernel(pg, pt_ref, q_ref, *refs):
    k_refs, v_refs = refs[:pg], refs[pg:2 * pg]
    o_ref, r_ref, acc_ref = refs[2 * pg:]
    j = pl.program_id(1)
    q8, own = _row_query(q_ref, N_HEADS, ATTN_SCALE)

    @pl.when(j == 0)
    def _():
        r_ref[...] = jnp.zeros(r_ref.shape, F32)
        acc_ref[...] = jnp.zeros(acc_ref.shape, F32)

    z = _dot(q8.astype(BF16), _gather_pages(k_refs).astype(BF16))
    ls = _log_sigmoid(z)
    keep = ls - z
    incl = _suffix_sum_lanes(keep)
    r_old = r_ref[:, 0:1]
    a = jnp.exp(ls + (incl - keep) + r_old)
    r_ref[:, 0:1] = r_old + incl[:, 0:1]
    acc_ref[...] += _dot_nt(a.astype(BF16), _gather_pages(v_refs).astype(BF16))

    @pl.when(j == pl.num_programs(1) - 1)
    def _():
        o_ref[...] = jnp.sum(jnp.where(own, acc_ref[...], 0.0), axis=0, keepdims=True)


def _decode_attention(kind, page_table, page_base, q, caches, extra, lam_init=None):
    db, n_pages = page_table.shape
    pg = PAGES_PER_STEP if n_pages % PAGES_PER_STEP == 0 else n_pages
    steps = n_pages // pg

    def page(rows, t):
        return pl.BlockSpec((None, rows, PAGE_SIZE),
                            lambda b, j, pt: (page_base + pt[b, (steps - 1 - j) * pg + t], 0, 0))

    row = pl.BlockSpec((None, 1, BRANCH_W), lambda b, j, pt: (b, 0, 0))
    small = lambda shape: pl.BlockSpec(shape, lambda b, j, pt: (0,) * len(shape))
    in_specs, args = [row], [q]
    state = pltpu.VMEM((SUBLANES, LANES), F32)
    acc = pltpu.VMEM((SUBLANES, BRANCH_W), F32)
    if kind == "fox":
        kn, vn, lfn = extra
        in_specs += [row, row, pl.BlockSpec((None, N_HEADS, 1), lambda b, j, pt: (b, 0, 0))]
        args += [kn, vn, lfn]
        kernel = functools.partial(_fox_decode_kernel, pg)
        scratch = [state, state, acc, state, pltpu.VMEM((SUBLANES, pg * PAGE_SIZE), F32)]
    elif kind == "diff":
        kn, vn, lams = extra
        in_specs += [row, row] + [small((1, DIFF_DC))] * 4
        args += [kn, vn, *lams]
        kernel, scratch = functools.partial(_diff_decode_kernel, pg, lam_init), [state, state, acc]
    else:
        kernel, scratch = functools.partial(_sb_decode_kernel, pg), [state, acc]
    for c in caches:
        in_specs += [page(c.shape[1], t) for t in range(pg)]
        args += [c] * pg
    return _call(kernel, grid=(db, steps), in_specs=in_specs, out_specs=row,
                 out_shape=jax.ShapeDtypeStruct((db, 1, BRANCH_W), F32), scratch=scratch, prefetch=1,
                 name=kind + "_decode")(page_table, *args)


def _layer_params(l, g_mix_pre, g_mix_post, g_ffn_pre, g_ffn_post, g_mem, w_in, b_fox_f, g_diff, w_mem_k, w_mem_v,
                  w_branch, w_gate, b_gate, w_out, w_ffn_in, conv_w, conv_b, w_ffn_out):
    row = lambda a: a[l][None, :].astype(F32)
    wt = jnp.transpose(w_in[l]).astype(BF16)
    q_groups = (0, 3, 6, 9)
    kv_groups = (1, 2, 4, 5, 7, 8)
    grp = lambda g: wt[g * BRANCH_W:(g + 1) * BRANCH_W]
    wft = jnp.zeros((SUBLANES, wt.shape[1]), BF16).at[:N_HEADS].set(wt[10 * BRANCH_W:])
    bf = jnp.zeros((SUBLANES, 1), F32).at[:N_HEADS, 0].set(b_fox_f[l])
    head = jnp.arange(BRANCH_W) // HEAD_DIM
    w_main = w_in[l][:, :10 * BRANCH_W].astype(BF16)
    return dict(
        g_mix_pre=row(g_mix_pre), g_mix_post=row(g_mix_post), g_ffn_pre=row(g_ffn_pre), g_ffn_post=row(g_ffn_post),
        g_mem=row(g_mem), g_diff=jnp.tile(g_diff[l], N_HEADS)[None, :].astype(F32),
        wq=jnp.concatenate([w_main[:, g * BRANCH_W:(g + 1) * BRANCH_W] for g in q_groups], axis=1),
        wkvt=jnp.concatenate([grp(g) for g in kv_groups], axis=0),
        w_all=jnp.concatenate([w_main[:, g * BRANCH_W:(g + 1) * BRANCH_W] for g in q_groups + kv_groups], axis=1),
        wft=wft, bf=bf,
        wmkt=jnp.transpose(w_mem_k[l]).astype(BF16), wmvt=jnp.transpose(w_mem_v[l]).astype(BF16),
        hsum=(head[:, None] == head[None, :]).astype(BF16),
        wb=w_branch[l].astype(BF16), wg=w_gate[l].astype(BF16), bg=row(b_gate), wo=w_out[l].astype(BF16),
        wfi=w_ffn_in[l].astype(BF16), cw=conv_w[l].astype(F32), cb=row(conv_b), wfo=w_ffn_out[l].astype(BF16))


def _feature_major_pages(cache):
    d, n, ps, h, e = cache.shape
    return jnp.transpose(cache, (0, 1, 3, 4, 2)).reshape(d * n, h * e, ps)


def _kv_out(per_layer):
    a = jnp.stack(per_layer)
    d, b, _, s = a.shape
    return jnp.transpose(a.reshape(d, b, N_HEADS, HEAD_DIM, s), (0, 1, 4, 2, 3))


def kernel(x_prompt, mem_prompt, x_sample, cache_fox_k, cache_fox_v, cache_fox_logf, cache_diff_k, cache_diff_v,
           cache_sb_k, cache_sb_v, cache_mem_k, cache_mem_v, state_conv, page_table, g_mix_pre, g_mix_post,
           g_ffn_pre, g_ffn_post, g_mem, w_in, b_fox_f, diff_lq1, diff_lk1, diff_lq2, diff_lk2, g_diff, w_mem_k,
           w_mem_v, w_branch, w_gate, b_gate, w_out, w_ffn_in, conv_w, conv_b, w_ffn_out):
    depth = w_in.shape[0]
    batch, seq, d_model = x_prompt.shape
    db = x_sample.shape[0]
    n_pool = cache_fox_k.shape[1]
    past_len = page_table.shape[1] * PAGE_SIZE
    n_mem = cache_mem_k.shape[2]
    assert x_sample.shape[1] == 1 and seq % ATT_BLOCK == 0

    kv_caches = [_feature_major_pages(c) for c in (cache_fox_k, cache_fox_v, cache_diff_k, cache_diff_v,
                                                   cache_sb_k, cache_sb_v)]
    lf_cache = jnp.transpose(cache_fox_logf, (0, 1, 3, 2)).reshape(depth * n_pool, N_HEADS, PAGE_SIZE)
    mem_kt = jnp.transpose(cache_mem_k, (0, 1, 3, 4, 2)).reshape(depth * db, BRANCH_W, n_mem)
    mem_vt = jnp.transpose(cache_mem_v, (0, 1, 3, 4, 2)).reshape(depth * db, BRANCH_W, n_mem)
    later = (jnp.arange(ATT_BLOCK)[:, None] > jnp.arange(ATT_BLOCK)[None, :]).astype(BF16)

    xp = x_prompt.reshape(batch * seq, d_model)
    xs = x_sample.reshape(db, d_model)
    p_rows, s_rows = [], []
    for l in range(depth):
        lam_init = 0.8 - 0.6 * math.exp(-0.3 * l)
        p = _layer_params(l, g_mix_pre, g_mix_post, g_ffn_pre, g_ffn_post, g_mem, w_in, b_fox_f, g_diff, w_mem_k,
                          w_mem_v, w_branch, w_gate, b_gate, w_out, w_ffn_in, conv_w, conv_b, w_ffn_out)
        lams = [a[l][None, :].astype(F32) for a in (diff_lq1, diff_lk1, diff_lq2, diff_lk2)]

        fq, dq, sq, mq, fkt, fvt, dkt, dvt, skt, svt, lft = _project_prompt(xp, batch, seq, p)
        as_seq = lambda a: a.reshape(batch, seq, BRANCH_W)
        e = _fox_suffix(lft)
        ecol = jnp.transpose(e[:, :N_HEADS, :], (0, 2, 1))
        fox_o = _prompt_attention("fox", as_seq(fq), fkt, fvt, (e, ecol))
        diff_o = _prompt_attention("diff", as_seq(dq), dkt, dvt, tuple(lams), lam_init)
        sb_o = _prompt_attention("sb", as_seq(sq), skt, svt, (later,))
        mkt, mvt = _mem_kv(mem_prompt, p)
        mem_o = _mem_attention(as_seq(mq), mkt, mvt, ATT_BLOCK, 0)
        flat = lambda a: a.reshape(batch * seq, BRANCH_W)
        hp = _merge(xp, flat(fox_o), flat(diff_o), flat(sb_o), flat(mem_o), p, lam_init)
        xp, conv_p = _ffn_prompt(hp, batch, seq, p)
        p_rows.append((fkt, fvt, lft[:, :N_HEADS, :], dkt, dvt, skt, svt, mkt, mvt, conv_p))

        sfq, sdq, ssq, smq, sfk, sfv, sdk, sdv, ssk, ssv, slf = _project_decode(xs, past_len, p)
        one = lambda a: a.reshape(db, 1, BRANCH_W)
        slf4 = jnp.transpose(slf[:N_HEADS, :])
        base = l * n_pool
        fox_s = _decode_attention("fox", page_table, base, one(sfq), [lf_cache, kv_caches[0], kv_caches[1]],
                                  (one(sfk), one(sfv), slf4[:, :, None]))
        diff_s = _decode_attention("diff", page_table, base, one(sdq), [kv_caches[2], kv_caches[3]],
                                   (one(sdk), one(sdv), lams), lam_init)
        sb_s = _decode_attention("sb", page_table, base, one(ssq), [kv_caches[4], kv_caches[5]], ())
        smq8 = jnp.zeros((db, SUBLANES, BRANCH_W), F32).at[:, 0, :].set(smq)
        mem_s = _mem_attention(smq8, mem_kt, mem_vt, SUBLANES, l * db)[:, 0, :]
        two = lambda a: a.reshape(db, BRANCH_W)
        hs = _merge(xs, two(fox_s), two(diff_s), two(sb_s), mem_s, p, lam_init)
        xs, conv_s = _ffn_decode(hs, state_conv[l], p)
        heads = lambda a: a.reshape(db, 1, N_HEADS, HEAD_DIM)
        s_rows.append((heads(sfk), heads(sfv), slf4[:, None, :], heads(sdk), heads(sdv), heads(ssk), heads(ssv), conv_s))

    pz = list(zip(*p_rows))
    p_fk, p_fv, p_dk, p_dv, p_sk, p_sv = (_kv_out(pz[i]) for i in (0, 1, 3, 4, 5, 6))
    p_fl = jnp.transpose(jnp.stack(pz[2]), (0, 1, 3, 2))
    p_mk, p_mv = _kv_out(pz[7]), _kv_out(pz[8])
    p_cv = jnp.stack(pz[9])
    s_out = [jnp.stack(a) for a in zip(*s_rows)]
    return (xp.reshape(batch, seq, d_model), xs.reshape(db, 1, d_model), p_fk, p_fv, p_fl, p_dk, p_dv, p_sk, p_sv,
            p_mk, p_mv, p_cv, *s_out)
```

```python
import functools
import math

import jax
import jax.numpy as jnp
from jax import lax
from jax.experimental import pallas as pl
from jax.experimental.pallas import tpu as pltpu

F32 = jnp.float32
BF16 = jnp.bfloat16

HEAD_DIM = 64
N_HEADS = 4
BRANCH_W = N_HEADS * HEAD_DIM
DIFF_DC = HEAD_DIM // 2
N_BRANCH = 4
PAGE_SIZE = 128
ROPE_THETA = 500000.0
ROPE_ROT = DIFF_DC // 4
RMS_EPS = 1e-6
ATTN_SCALE = HEAD_DIM ** -0.5
DIFF_SCALE = DIFF_DC ** -0.5
NEG = -1e30

LANES = 128
SUBLANES = 8
VMEM_LIMIT_BYTES = 56 * 1024 * 1024
ATT_BLOCK = 256
FF_CHUNK = 256
PAGES_PER_STEP = 16


def _call(kernel, *, grid, in_specs, out_specs, out_shape, scratch=(), prefetch=0, name):
    spec = pltpu.PrefetchScalarGridSpec(num_scalar_prefetch=prefetch, grid=grid, in_specs=in_specs,
                                        out_specs=out_specs, scratch_shapes=list(scratch))
    params = pltpu.CompilerParams(dimension_semantics=("arbitrary",) * len(grid),
                                  vmem_limit_bytes=VMEM_LIMIT_BYTES)
    return pl.pallas_call(kernel, grid_spec=spec, out_shape=out_shape, compiler_params=params, name=name)


def _resident(shape):
    zeros = (0,) * len(shape)
    return pl.BlockSpec(shape, lambda *_: zeros, pipeline_mode=pl.Buffered(1))


def _rms(x, g):
    return x * lax.rsqrt(jnp.mean(x * x, axis=-1, keepdims=True) + RMS_EPS) * g


def _log_sigmoid(z):
    return jnp.minimum(z, 0.0) - jnp.log(1.0 + jnp.exp(-jnp.abs(z)))


def _dot(a, b):
    return jnp.dot(a, b, preferred_element_type=F32)


def _dot_nt(a, b):
    return lax.dot_general(a, b, (((1,), (1,)), ((), ())), preferred_element_type=F32)


def _split_bf16(x):
    hi = x.astype(BF16)
    lo = (x - hi.astype(F32)).astype(BF16)
    return hi, lo


def _group_mask(shape, axis, width, g):
    idx = lax.broadcasted_iota(jnp.int32, shape, axis)
    return (idx >= g * width) & (idx < (g + 1) * width)


def _set_row(state, h, row):
    return jnp.where(lax.broadcasted_iota(jnp.int32, state.shape, 0) == h, row, state)


def _rope_rows(h, c, s1, s2):
    return h * c + pltpu.roll(h, BRANCH_W - ROPE_ROT // 2, 1) * s1 + pltpu.roll(h, ROPE_ROT // 2, 1) * s2


def _rope_cols(h, c, s1, s2):
    return h * c + pltpu.roll(h, BRANCH_W - ROPE_ROT // 2, 0) * s1 + pltpu.roll(h, ROPE_ROT // 2, 0) * s2


def _proj_prompt_kernel(x_ref, g_ref, wmq_ref, wt_ref, wft_ref, bf_ref, rct_ref, rs1t_ref, rs2t_ref,
                        mq_ref, fq_ref, dq_ref, sq_ref, fk_ref, fv_ref, dk_ref, dv_ref, sk_ref, sv_ref, lf_ref):
    xn = _rms(x_ref[...], g_ref[...]).astype(BF16)
    mq_ref[...] = _dot(xn, wmq_ref[...])
    outs = (fq_ref, dq_ref, sq_ref, fk_ref, fv_ref, dk_ref, dv_ref, sk_ref, sv_ref)
    for gi, out in enumerate(outs):
        ht = _dot_nt(wt_ref[gi * BRANCH_W:(gi + 1) * BRANCH_W, :], xn)
        if out is dq_ref or out is dk_ref:
            ht = _rope_cols(ht, rct_ref[...], rs1t_ref[...], rs2t_ref[...])
        out[0] = ht
    lf_ref[0] = _log_sigmoid(_dot_nt(wft_ref[...], xn) + bf_ref[...])


def _proj_decode_kernel(x_ref, g_ref, w_ref, wft_ref, bf_ref, rc_ref, rs1_ref, rs2_ref, *outs):
    xn = _rms(x_ref[...], g_ref[...]).astype(BF16)
    for gi in range(10):
        h = _dot(xn, w_ref[:, gi * BRANCH_W:(gi + 1) * BRANCH_W])
        if gi in (1, 6):
            h = _rope_rows(h, rc_ref[...], rs1_ref[...], rs2_ref[...])
        outs[gi][...] = h
    outs[10][...] = _log_sigmoid(_dot_nt(wft_ref[...], xn) + bf_ref[...])


def _rope_tables(pos):
    half = ROPE_ROT // 2
    inv_freq = ROPE_THETA ** (-jnp.arange(half, dtype=F32) * 2.0 / ROPE_ROT)
    ang = pos.astype(F32)[:, None] * inv_freq[None, :]
    cos, sin = jnp.cos(ang), jnp.sin(ang)
    n = pos.shape[0]
    pad = jnp.zeros((n, DIFF_DC - ROPE_ROT), F32)
    c = jnp.concatenate([cos, cos, pad + 1.0], axis=1)
    s1 = jnp.concatenate([-sin, jnp.zeros_like(sin), pad], axis=1)
    s2 = jnp.concatenate([jnp.zeros_like(sin), sin, pad], axis=1)
    reps = BRANCH_W // DIFF_DC
    return tuple(jnp.tile(t, (1, reps)) for t in (c, s1, s2))


def _project_prompt(x2, batch, seq, p):
    t = batch * seq
    tm = 512 if seq % 512 == 0 else seq
    per_seq = seq // tm
    rct, rs1t, rs2t = (a.T for a in _rope_tables(jnp.arange(seq)))
    d = x2.shape[1]
    row = lambda i: (i, 0)
    tabt = lambda i: (0, i % per_seq)
    fm = lambda i: (i // per_seq, 0, i % per_seq)
    in_specs = [pl.BlockSpec((tm, d), row), _resident((1, d)), _resident(p['wmq'].shape), _resident(p['wt'].shape),
                _resident(p['wft'].shape), _resident((SUBLANES, 1)),
                pl.BlockSpec((BRANCH_W, tm), tabt), pl.BlockSpec((BRANCH_W, tm), tabt), pl.BlockSpec((BRANCH_W, tm), tabt)]
    out_specs = ([pl.BlockSpec((tm, BRANCH_W), row)] + [pl.BlockSpec((1, BRANCH_W, tm), fm)] * 9
                 + [pl.BlockSpec((1, SUBLANES, tm), fm)])
    out_shape = ([jax.ShapeDtypeStruct((t, BRANCH_W), F32)]
                 + [jax.ShapeDtypeStruct((batch, BRANCH_W, seq), F32)] * 9
                 + [jax.ShapeDtypeStruct((batch, SUBLANES, seq), F32)])
    return _call(_proj_prompt_kernel, grid=(t // tm,), in_specs=in_specs, out_specs=out_specs, out_shape=out_shape,
                 name="proj_prompt")(x2, p['g_mix_pre'], p['wmq'], p['wt'], p['wft'], p['bf'], rct, rs1t, rs2t)


def _project_decode(x2, pos, p):
    m, d = x2.shape
    rc, rs1, rs2 = _rope_tables(jnp.full((m,), pos))
    full = lambda shape: pl.BlockSpec(shape, lambda i: (0,) * len(shape))
    in_specs = [full((m, d)), full((1, d)), full(p['w_all'].shape), full(p['wft'].shape), full((SUBLANES, 1)),
                full((m, BRANCH_W)), full((m, BRANCH_W)), full((m, BRANCH_W))]
    out_specs = [full((m, BRANCH_W))] * 10 + [full((SUBLANES, m))]
    out_shape = [jax.ShapeDtypeStruct((m, BRANCH_W), F32)] * 10 + [jax.ShapeDtypeStruct((SUBLANES, m), F32)]
    return _call(_proj_decode_kernel, grid=(1,), in_specs=in_specs, out_specs=out_specs, out_shape=out_shape,
                 name="proj_decode")(x2, p['g_mix_pre'], p['w_all'], p['wft'], p['bf'], rc, rs1, rs2)


def _suffix_sum_lanes(x):
    n = x.shape[1]
    lane = lax.broadcasted_iota(jnp.int32, x.shape, 1)
    d = 1
    while d < n:
        x = x + jnp.where(lane + d < n, pltpu.roll(x, n - d, 1), 0.0)
        d *= 2
    return x


def _suffix_kernel(lf_ref, e_ref):
    lf = lf_ref[0]
    e_ref[0] = _suffix_sum_lanes(lf) - lf


def _fox_suffix(lft):
    b, r, s = lft.shape
    spec = pl.BlockSpec((1, r, s), lambda i: (i, 0, 0))
    return _call(_suffix_kernel, grid=(b,), in_specs=[spec], out_specs=spec,
                 out_shape=jax.ShapeDtypeStruct(lft.shape, F32), name="fox_suffix")(lft)


def _build_q_stack(qt_ref, qs_ref, n_groups, scale, tq):
    nq = qs_ref.shape[0]
    width = BRANCH_W // n_groups
    for i in range(nq):
        qt = qt_ref[0, :, i * tq:(i + 1) * tq] * scale
        for g in range(n_groups):
            qs_ref[i, :, g * tq:(g + 1) * tq] = jnp.where(_group_mask(qt.shape, 0, width, g), qt, 0.0).astype(BF16)


def _visit_query_blocks(stages, first, nq):
    def run(blocks, masked):
        carries = [None] * len(blocks)
        for stage in stages:
            carries = [stage(i, c, masked) for i, c in zip(blocks, carries)]

    run([first], True)
    rest = nq - 1 - first
    odd = lax.rem(rest, 2)

    @pl.when(odd == 1)
    def _():
        run([first + 1], False)

    start = first + 1 + odd

    def pair(t, c):
        run([start + 2 * t, start + 2 * t + 1], False)
        return c

    lax.fori_loop(0, lax.div(rest, 2), pair, 0)


def _softmax_step(u, m_old, l_old, shift=None):
    mu = jnp.max(u, axis=0, keepdims=True)
    if shift is not None:
        mu = mu - shift
    m_new = jnp.maximum(m_old, mu)
    p = jnp.exp(u - (m_new if shift is None else m_new + shift))
    alpha = jnp.exp(m_old - m_new)
    return p.astype(BF16), alpha, m_new, alpha * l_old + jnp.sum(p, axis=0, keepdims=True)


def _head_rows(h):
    return slice(h * HEAD_DIM, (h + 1) * HEAD_DIM)


def _fox_prompt_kernel(qt_ref, kt_ref, vt_ref, ecol_ref, erow_ref, o_ref, qs_ref, acc_ref, m_ref, l_ref):
    tq = tk = ATT_BLOCK
    nq = qs_ref.shape[0]
    j = pl.program_id(1)

    @pl.when(j == 0)
    def _():
        _build_q_stack(qt_ref, qs_ref, N_HEADS, ATTN_SCALE, tq)
        m_ref[...] = jnp.full(m_ref.shape, NEG, F32)
        l_ref[...] = jnp.zeros(l_ref.shape, F32)
        acc_ref[...] = jnp.zeros(acc_ref.shape, F32)

    k = kt_ref[0].T.astype(BF16)
    vt = vt_ref[0].astype(BF16)
    ek = [jnp.broadcast_to(ecol_ref[0, :, h:h + 1], (tk, tq)) for h in range(N_HEADS)]
    causal = lax.broadcasted_iota(jnp.int32, (tk, tq), 0) <= lax.broadcasted_iota(jnp.int32, (tk, tq), 1)

    def scores(i, _, masked):
        return _dot(k, qs_ref[i])

    def update(i, s_all, masked):
        m_all, l_all, eq_all = m_ref[i], l_ref[i], erow_ref[0, i]
        for h in range(N_HEADS):
            u = s_all[:, h * tq:(h + 1) * tq] + ek[h]
            if masked:
                u = jnp.where(causal, u, NEG)
            p, alpha, m_new, l_new = _softmax_step(u, m_all[h:h + 1], l_all[h:h + 1], shift=eq_all[h:h + 1])
            acc_ref[i, _head_rows(h), :] = acc_ref[i, _head_rows(h), :] * alpha + _dot(vt[_head_rows(h), :], p)
            m_all, l_all = _set_row(m_all, h, m_new), _set_row(l_all, h, l_new)
        m_ref[i], l_ref[i] = m_all, l_all

    _visit_query_blocks((scores, update), j, nq)

    @pl.when(j == pl.num_programs(1) - 1)
    def _():
        for i in range(nq):
            l_all = l_ref[i]
            ot = jnp.concatenate([acc_ref[i, _head_rows(h), :] * (1.0 / l_all[h:h + 1]) for h in range(N_HEADS)], axis=0)
            o_ref[0, i * tq:(i + 1) * tq, :] = ot.T


def _diff_lambda(lq1_ref, lk1_ref, lq2_ref, lk2_ref, lam_init):
    e1 = jnp.exp(jnp.sum(lq1_ref[...] * lk1_ref[...], axis=1, keepdims=True))
    e2 = jnp.exp(jnp.sum(lq2_ref[...] * lk2_ref[...], axis=1, keepdims=True))
    return e1 - e2 + lam_init


def _diff_prompt_kernel(lam_init, qt_ref, kt_ref, vt_ref, lq1_ref, lk1_ref, lq2_ref, lk2_ref, o_ref,
                        qs_ref, acc1_ref, acc2_ref, m_ref, l_ref):
    tq = tk = ATT_BLOCK
    nq = qs_ref.shape[0]
    j = pl.program_id(1)

    @pl.when(j == 0)
    def _():
        _build_q_stack(qt_ref, qs_ref, 2 * N_HEADS, DIFF_SCALE, tq)
        m_ref[...] = jnp.full(m_ref.shape, NEG, F32)
        l_ref[...] = jnp.zeros(l_ref.shape, F32)
        acc1_ref[...] = jnp.zeros(acc1_ref.shape, F32)
        acc2_ref[...] = jnp.zeros(acc2_ref.shape, F32)

    k = kt_ref[0].T.astype(BF16)
    vt = vt_ref[0].astype(BF16)
    causal = lax.broadcasted_iota(jnp.int32, (tk, tq), 0) <= lax.broadcasted_iota(jnp.int32, (tk, tq), 1)

    def scores(i, _, masked):
        return _dot(k, qs_ref[i])

    def update(i, s_all, masked):
        m_all, l_all = m_ref[i], l_ref[i]
        for h in range(N_HEADS):
            ps, alphas = [], []
            for g in (2 * h, 2 * h + 1):
                u = s_all[:, g * tq:(g + 1) * tq]
                if masked:
                    u = jnp.where(causal, u, NEG)
                p, alpha, m_new, l_new = _softmax_step(u, m_all[g:g + 1], l_all[g:g + 1])
                m_all, l_all = _set_row(m_all, g, m_new), _set_row(l_all, g, l_new)
                ps.append(p)
                alphas.append(alpha)
            pv = _dot(vt[_head_rows(h), :], jnp.concatenate(ps, axis=1))
            acc1_ref[i, _head_rows(h), :] = acc1_ref[i, _head_rows(h), :] * alphas[0] + pv[:, :tq]
            acc2_ref[i, _head_rows(h), :] = acc2_ref[i, _head_rows(h), :] * alphas[1] + pv[:, tq:]
        m_ref[i], l_ref[i] = m_all, l_all

    _visit_query_blocks((scores, update), j, nq)

    @pl.when(j == pl.num_programs(1) - 1)
    def _():
        lam = _diff_lambda(lq1_ref, lk1_ref, lq2_ref, lk2_ref, lam_init)
        for i in range(nq):
            l_all = l_ref[i]
            ot = jnp.concatenate(
                [acc1_ref[i, _head_rows(h), :] * (1.0 / l_all[2 * h:2 * h + 1])
                 - lam * (acc2_ref[i, _head_rows(h), :] * (1.0 / l_all[2 * h + 1:2 * h + 2])) for h in range(N_HEADS)], axis=0)
            o_ref[0, i * tq:(i + 1) * tq, :] = ot.T


def _sb_prompt_kernel(qt_ref, kt_ref, vt_ref, later_ref, o_ref, qs_ref, acc_ref, r_ref):
    tq = tk = ATT_BLOCK
    nq = qs_ref.shape[0]
    j = pl.program_id(1)
    jb = pl.num_programs(1) - 1 - j

    @pl.when(j == 0)
    def _():
        _build_q_stack(qt_ref, qs_ref, N_HEADS, ATTN_SCALE, tq)
        r_ref[...] = jnp.zeros(r_ref.shape, F32)
        acc_ref[...] = jnp.zeros(acc_ref.shape, F32)

    k = kt_ref[0].T.astype(BF16)
    vt = vt_ref[0].astype(BF16)
    later = later_ref[...]
    strict = lax.broadcasted_iota(jnp.int32, (tk, tq), 0) < lax.broadcasted_iota(jnp.int32, (tk, tq), 1)

    def scores(i, _, masked):
        return _dot(k, qs_ref[i])

    def gates(i, z_all, masked):
        r_all = r_ref[i]
        log_beta, keeps = [], []
        for h in range(N_HEADS):
            z = z_all[:, h * tq:(h + 1) * tq]
            ls = _log_sigmoid(z)
            keep = ls - z
            if masked:
                keep = jnp.where(strict, keep, 0.0)
            keeps.append(keep.astype(BF16))
            log_beta.append(ls + r_all[h:h + 1])
            r_all = _set_row(r_all, h, r_all[h:h + 1] + jnp.sum(keep, axis=0, keepdims=True))
        r_ref[i] = r_all
        tails = _dot(later, jnp.concatenate(keeps, axis=1))
        return log_beta, tails

    def accumulate(i, carry, masked):
        log_beta, tails = carry
        for h in range(N_HEADS):
            a = jnp.exp(log_beta[h] + tails[:, h * tq:(h + 1) * tq])
            if masked:
                a = jnp.where(strict, a, 0.0)
            acc_ref[i, _head_rows(h), :] += _dot(vt[_head_rows(h), :], a.astype(BF16))

    _visit_query_blocks((scores, gates, accumulate), jb, nq)

    @pl.when(j == pl.num_programs(1) - 1)
    def _():
        for i in range(nq):
            o_ref[0, i * tq:(i + 1) * tq, :] = acc_ref[i].T


def _prompt_attention(kind, qt, kt, vt, extra, lam_init=None):
    b, _, s = qt.shape
    tq = tk = ATT_BLOCK
    nq, nk = s // tq, s // tk
    kblk = (lambda bi, j: nk - 1 - j) if kind == "sb" else (lambda bi, j: j)
    whole = pl.BlockSpec((1, BRANCH_W, s), lambda bi, j: (bi, 0, 0))
    kv_spec = pl.BlockSpec((1, BRANCH_W, tk), lambda bi, j: (bi, 0, kblk(bi, j)))
    in_specs = [whole, kv_spec, kv_spec]
    state = pltpu.VMEM((nq, SUBLANES, tq), F32)
    acc = pltpu.VMEM((nq, BRANCH_W, tq), F32)
    if kind == "fox":
        in_specs += [pl.BlockSpec((1, tk, N_HEADS), lambda bi, j: (bi, j, 0)),
                     pl.BlockSpec((1, nq, SUBLANES, tq), lambda bi, j: (bi, 0, 0, 0))]
        kernel, groups, scratch = _fox_prompt_kernel, N_HEADS, [acc, state, state]
    elif kind == "diff":
        in_specs += [_resident((1, DIFF_DC))] * 4
        kernel, groups, scratch = functools.partial(_diff_prompt_kernel, lam_init), 2 * N_HEADS, [acc, acc, state, state]
    else:
        in_specs += [_resident((tk, tk))]
        kernel, groups, scratch = _sb_prompt_kernel, N_HEADS, [acc, state]
    scratch = [pltpu.VMEM((nq, BRANCH_W, groups * tq), BF16)] + scratch
    return _call(kernel, grid=(b, nk), in_specs=in_specs,
                 out_specs=pl.BlockSpec((1, s, BRANCH_W), lambda bi, j: (bi, 0, 0)),
                 out_shape=jax.ShapeDtypeStruct((b, s, BRANCH_W), F32), scratch=scratch,
                 name=kind + "_prompt")(qt, kt, vt, *extra)


def _mem_kv_kernel(mem_ref, g_ref, wkt_ref, wvt_ref, mk_ref, mv_ref):
    mn = _rms(mem_ref[0], g_ref[...]).astype(BF16)
    mk_ref[0] = _dot_nt(wkt_ref[...], mn)
    mv_ref[0] = _dot_nt(wvt_ref[...], mn)


def _mem_kv(mem, p):
    b, n, d = mem.shape
    out = pl.BlockSpec((1, BRANCH_W, n), lambda i: (i, 0, 0))
    return _call(_mem_kv_kernel, grid=(b,),
                 in_specs=[pl.BlockSpec((1, n, d), lambda i: (i, 0, 0)), _resident((1, d)),
                           _resident((BRANCH_W, d)), _resident((BRANCH_W, d))],
                 out_specs=[out, out], out_shape=[jax.ShapeDtypeStruct((b, BRANCH_W, n), F32)] * 2,
                 name="mem_kv")(mem, p['g_mem'], p['wmkt'], p['wmvt'])


def _per_head_lanes(cols):
    tq = cols[0].shape[0]
    lane = lax.broadcasted_iota(jnp.int32, (tq, LANES), 1)
    lo = jnp.where(lane < HEAD_DIM, cols[0], cols[1])
    hi = jnp.where(lane < HEAD_DIM, cols[2], cols[3])
    return jnp.concatenate([lo, hi], axis=1)


def _mem_attn_kernel(q_ref, kt_ref, vt_ref, o_ref):
    tq = q_ref.shape[1]
    q = q_ref[0] * ATTN_SCALE
    qs = jnp.concatenate([jnp.where(_group_mask(q.shape, 1, HEAD_DIM, h), q, 0.0) for h in range(N_HEADS)], axis=0)
    s_all = _dot(qs.astype(BF16), kt_ref[0].astype(BF16))
    vt = vt_ref[0].astype(BF16)
    vbd = jnp.concatenate([jnp.where(_group_mask(vt.shape, 0, HEAD_DIM, h), vt, jnp.zeros_like(vt))
                           for h in range(N_HEADS)], axis=1)
    ps, inv = [], []
    for h in range(N_HEADS):
        s = s_all[h * tq:(h + 1) * tq]
        p = jnp.exp(s - jnp.max(s, axis=1, keepdims=True))
        inv.append(1.0 / jnp.sum(p, axis=1, keepdims=True))
        ps.append(p.astype(BF16))
    o_ref[0] = _dot_nt(jnp.concatenate(ps, axis=1), vbd) * _per_head_lanes(inv)


def _mem_attention(q, kt, vt, tq, kv_base):
    b, s, _ = q.shape
    n = kt.shape[2]
    qspec = pl.BlockSpec((1, tq, BRANCH_W), lambda bi, i: (bi, i, 0))
    kvspec = pl.BlockSpec((1, BRANCH_W, n), lambda bi, i: (kv_base + bi, 0, 0))
    return _call(_mem_attn_kernel, grid=(b, s // tq), in_specs=[qspec, kvspec, kvspec], out_specs=qspec,
                 out_shape=jax.ShapeDtypeStruct(q.shape, F32), name="mem_attn")(q, kt, vt)


def _merge_kernel(lam_init, x_ref, fo_ref, do_ref, so_ref, mo_ref, gpre_ref, gdiff_ref, hsum_ref,
                  wb_ref, wg_ref, bg_ref, wo_ref, gpost_ref, out_ref):
    x = x_ref[...]
    d_model = x.shape[1]
    xn = _rms(x, gpre_ref[...]).astype(BF16)
    d = do_ref[...]
    hi, lo = _split_bf16(d * d)
    ms = (_dot(hi, hsum_ref[...]) + _dot(lo, hsum_ref[...])) * (1.0 / HEAD_DIM)
    dn = d * lax.rsqrt(ms + RMS_EPS) * gdiff_ref[...] * (1.0 - lam_init)
    acc = jnp.zeros((x.shape[0], d_model), F32)
    for n, o in enumerate((fo_ref[...], dn, so_ref[...], mo_ref[...])):
        proj = _dot(o.astype(BF16), wb_ref[n])
        gate = jax.nn.sigmoid(_dot(xn, wg_ref[:, n * d_model:(n + 1) * d_model]) + bg_ref[:, n * d_model:(n + 1) * d_model])
        acc = acc + gate * proj
    out_ref[...] = x + _rms(_dot(acc.astype(BF16), wo_ref[...]), gpost_ref[...])


def _merge(x2, fo, do, so, mo, p, lam_init):
    m, d = x2.shape
    tm = 256 if m % 256 == 0 else m
    row = lambda w: pl.BlockSpec((tm, w), lambda i: (i, 0))
    in_specs = [row(d)] + [row(BRANCH_W)] * 4 + [_resident((1, d)), _resident((1, BRANCH_W)), _resident((BRANCH_W, BRANCH_W)),
                                                  _resident(p['wb'].shape), _resident(p['wg'].shape), _resident(p['bg'].shape),
                                                  _resident(p['wo'].shape), _resident((1, d))]
    return _call(functools.partial(_merge_kernel, lam_init), grid=(m // tm,), in_specs=in_specs, out_specs=row(d),
                 out_shape=jax.ShapeDtypeStruct((m, d), F32), name="merge")(
        x2, fo, do, so, mo, p['g_mix_pre'], p['g_diff'], p['hsum'], p['wb'], p['wg'], p['bg'], p['wo'], p['g_mix_post'])


def _ffn_cols(c):
    return slice(c * FF_CHUNK, (c + 1) * FF_CHUNK)


def _ffn_up(hn, c, win_ref, d_ff):
    return (_dot(hn, win_ref[:, _ffn_cols(c)]),
            _dot(hn, win_ref[:, d_ff + c * FF_CHUNK:d_ff + (c + 1) * FF_CHUNK]))


def _ffn_down(a, a1, a2, u, c, cw_ref, cb_ref, wout_ref):
    cols = _ffn_cols(c)
    conv = cw_ref[0:1, cols] * a2 + cw_ref[1:2, cols] * a1 + cw_ref[2:3, cols] * a + cb_ref[:, cols]
    y = jax.nn.gelu(conv, approximate=True) * u
    return _dot(y.astype(BF16), wout_ref[cols, :])


def _ffn_prompt_kernel(per_seq, h_ref, gpre_ref, win_ref, cw_ref, cb_ref, wout_ref, gpost_ref, out_ref, conv_ref, carry_ref):
    h = h_ref[...]
    tm = h.shape[0]
    d_ff = cw_ref.shape[1]
    n_chunks = d_ff // FF_CHUNK
    hn = _rms(h, gpre_ref[...]).astype(BF16)

    @pl.when(pl.program_id(0) % per_seq == 0)
    def _():
        carry_ref[...] = jnp.zeros(carry_ref.shape, F32)

    row = lax.broadcasted_iota(jnp.int32, (tm, FF_CHUNK), 0)
    f = jnp.zeros(h.shape, F32)
    nxt = _ffn_up(hn, 0, win_ref, d_ff)
    for c in range(n_chunks):
        a, u = nxt
        if c + 1 < n_chunks:
            nxt = _ffn_up(hn, c + 1, win_ref, d_ff)
        cols = _ffn_cols(c)
        pm2, pm1 = carry_ref[0:1, cols], carry_ref[1:2, cols]
        a1 = jnp.where(row == 0, pm1, pltpu.roll(a, 1, 0))
        a2 = jnp.where(row == 0, pm2, jnp.where(row == 1, pm1, pltpu.roll(a, 2, 0)))
        f = f + _ffn_down(a, a1, a2, u, c, cw_ref, cb_ref, wout_ref)
        carry_ref[0:2, cols] = a[tm - 2:tm, :]
        conv_ref[0, :, cols] = a[tm - 2:tm, :]
    out_ref[...] = h + _rms(f, gpost_ref[...])


def _ffn_decode_kernel(h_ref, gpre_ref, win_ref, cw_ref, cb_ref, wout_ref, gpost_ref, p0_ref, p1_ref, out_ref, a_ref):
    h = h_ref[...]
    d_ff = cw_ref.shape[1]
    hn = _rms(h, gpre_ref[...]).astype(BF16)
    f = jnp.zeros(h.shape, F32)
    for c in range(d_ff // FF_CHUNK):
        cols = _ffn_cols(c)
        a, u = _ffn_up(hn, c, win_ref, d_ff)
        f = f + _ffn_down(a, p1_ref[:, cols], p0_ref[:, cols], u, c, cw_ref, cb_ref, wout_ref)
        a_ref[:, cols] = a
    out_ref[...] = h + _rms(f, gpost_ref[...])


def _ffn_weights_specs(p, d):
    return [_resident((1, d)), _resident(p['wfi'].shape), _resident(p['cw'].shape), _resident(p['cb'].shape),
            _resident(p['wfo'].shape), _resident((1, d))]


def _ffn_prompt(h2, batch, seq, p):
    m, d = h2.shape
    d_ff = p['cw'].shape[1]
    tm = 256 if seq % 256 == 0 else seq
    per_seq = seq // tm
    row = pl.BlockSpec((tm, d), lambda i: (i, 0))
    return _call(functools.partial(_ffn_prompt_kernel, per_seq), grid=(m // tm,),
                 in_specs=[row] + _ffn_weights_specs(p, d),
                 out_specs=[row, pl.BlockSpec((1, 2, d_ff), lambda i: (i // per_seq, 0, 0))],
                 out_shape=[jax.ShapeDtypeStruct((m, d), F32), jax.ShapeDtypeStruct((batch, 2, d_ff), F32)],
                 scratch=[pltpu.VMEM((SUBLANES, d_ff), F32)], name="ffn_prompt")(
        h2, p['g_ffn_pre'], p['wfi'], p['cw'], p['cb'], p['wfo'], p['g_ffn_post'])


def _ffn_decode(h2, prev, p):
    m, d = h2.shape
    d_ff = p['cw'].shape[1]
    full = lambda shape: pl.BlockSpec(shape, lambda i: (0,) * len(shape))
    y, a = _call(_ffn_decode_kernel, grid=(1,),
                 in_specs=[full((m, d))] + _ffn_weights_specs(p, d) + [full((m, d_ff)), full((m, d_ff))],
                 out_specs=[full((m, d)), full((m, d_ff))],
                 out_shape=[jax.ShapeDtypeStruct((m, d), F32), jax.ShapeDtypeStruct((m, d_ff), F32)],
                 name="ffn_decode")(h2, p['g_ffn_pre'], p['wfi'], p['cw'], p['cb'], p['wfo'], p['g_ffn_post'],
                                    prev[:, 0], prev[:, 1])
    return y, jnp.stack([prev[:, 1], a], axis=1)


def _row_query(q_ref, n_groups, scale):
    shape = (SUBLANES, BRANCH_W)
    width = BRANCH_W // n_groups
    own = lax.broadcasted_iota(jnp.int32, shape, 1) // width == lax.broadcasted_iota(jnp.int32, shape, 0)
    return jnp.where(own, jnp.broadcast_to(q_ref[...] * scale, shape), 0.0), own


def _gather_pages(refs):
    return jnp.concatenate([r[...] for r in refs], axis=1).astype(BF16)


def _decode_softmax_step(s, vt, m_ref, l_ref, acc_ref):
    m_old = m_ref[:, 0:1]
    m_new = jnp.maximum(m_old, jnp.max(s, axis=1, keepdims=True))
    p = jnp.exp(s - m_new)
    alpha = jnp.exp(m_old - m_new)
    l_ref[:, 0:1] = alpha * l_ref[:, 0:1] + jnp.sum(p, axis=1, keepdims=True)
    m_ref[:, 0:1] = m_new
    acc_ref[...] = alpha * acc_ref[...] + _dot_nt(p.astype(BF16), vt)


def _decode_self_init(q8, kn_ref, vn_ref, m_ref, l_ref, acc_ref):
    m_ref[:, 0:1] = jnp.sum(q8 * kn_ref[...], axis=1, keepdims=True)
    l_ref[:, 0:1] = jnp.ones((SUBLANES, 1), F32)
    acc_ref[...] = jnp.broadcast_to(vn_ref[...], acc_ref.shape)


def _decode_kernel(pg, lam_init, pt_ref, fq_ref, dq_ref, sq_ref, fkn_ref, fvn_ref, dkn_ref, dvn_ref, lfn_ref,
                   lq1_ref, lk1_ref, lq2_ref, lk2_ref, *refs):
    lf_refs, fk_refs, fv_refs, dk_refs, dv_refs, sk_refs, sv_refs = (refs[n * pg:(n + 1) * pg] for n in range(7))
    (fo_ref, do_ref, so_ref, fm_ref, fl_ref, facc_ref, fc_ref, lf_ref,
     dm_ref, dl_ref, dacc_ref, sr_ref, sacc_ref) = refs[7 * pg:]
    j = pl.program_id(1)
    first, last = j == 0, j == pl.num_programs(1) - 1
    fq8, f_own = _row_query(fq_ref, N_HEADS, ATTN_SCALE)
    dq8, _ = _row_query(dq_ref, 2 * N_HEADS, DIFF_SCALE)
    sq8, s_own = _row_query(sq_ref, N_HEADS, ATTN_SCALE)

    @pl.when(first)
    def _():
        _decode_self_init(fq8, fkn_ref, fvn_ref, fm_ref, fl_ref, facc_ref)
        _decode_self_init(dq8, dkn_ref, dvn_ref, dm_ref, dl_ref, dacc_ref)
        fc_ref[...] = jnp.zeros(fc_ref.shape, F32)
        fc_ref[0:N_HEADS, 0:1] = lfn_ref[...]
        lf_ref[...] = jnp.zeros(lf_ref.shape, F32)
        sr_ref[...] = jnp.zeros(sr_ref.shape, F32)
        sacc_ref[...] = jnp.zeros(sacc_ref.shape, F32)

    for t in range(pg):
        lf_ref[0:N_HEADS, t * PAGE_SIZE:(t + 1) * PAGE_SIZE] = lf_refs[t][...]
    lf = lf_ref[...]
    incl = _suffix_sum_lanes(lf)
    carry = fc_ref[:, 0:1]
    fc_ref[:, 0:1] = carry + incl[:, 0:1]
    s = _dot(fq8.astype(BF16), _gather_pages(fk_refs)) + (incl - lf + carry)
    _decode_softmax_step(s, _gather_pages(fv_refs), fm_ref, fl_ref, facc_ref)

    s = _dot(dq8.astype(BF16), _gather_pages(dk_refs))
    _decode_softmax_step(s, _gather_pages(dv_refs), dm_ref, dl_ref, dacc_ref)

    z = _dot(sq8.astype(BF16), _gather_pages(sk_refs))
    ls = _log_sigmoid(z)
    keep = ls - z
    incl = _suffix_sum_lanes(keep)
    r_old = sr_ref[:, 0:1]
    a = jnp.exp(ls + (incl - keep) + r_old)
    sr_ref[:, 0:1] = r_old + incl[:, 0:1]
    sacc_ref[...] += _dot_nt(a.astype(BF16), _gather_pages(sv_refs))

    @pl.when(last)
    def _():
        fo_ref[...] = jnp.sum(jnp.where(f_own, facc_ref[...] / fl_ref[:, 0:1], 0.0), axis=0, keepdims=True)
        lam = _diff_lambda(lq1_ref, lk1_ref, lq2_ref, lk2_ref, lam_init)
        shape = (SUBLANES, BRANCH_W)
        row = lax.broadcasted_iota(jnp.int32, shape, 0)
        head = lax.broadcasted_iota(jnp.int32, shape, 1) // HEAD_DIM
        o8 = dacc_ref[...] / dl_ref[:, 0:1] * jnp.where(row % 2 == 0, 1.0, -lam)
        do_ref[...] = jnp.sum(jnp.where(head == row // 2, o8, 0.0), axis=0, keepdims=True)
        so_ref[...] = jnp.sum(jnp.where(s_own, sacc_ref[...], 0.0), axis=0, keepdims=True)


def _decode_attention(page_table, page_base, queries, new_kv, lfn, lams, lf_cache, kv_caches, lam_init):
    db, n_pages = page_table.shape
    pg = PAGES_PER_STEP if n_pages % PAGES_PER_STEP == 0 else n_pages
    steps = n_pages // pg

    def page(rows, t):
        return pl.BlockSpec((None, rows, PAGE_SIZE),
                            lambda b, j, pt: (page_base + pt[b, (steps - 1 - j) * pg + t], 0, 0))

    row = pl.BlockSpec((None, 1, BRANCH_W), lambda b, j, pt: (b, 0, 0))
    small = lambda shape: pl.BlockSpec(shape, lambda b, j, pt: (0,) * len(shape))
    in_specs = [row] * 7 + [pl.BlockSpec((None, N_HEADS, 1), lambda b, j, pt: (b, 0, 0))] + [small((1, DIFF_DC))] * 4
    args = [*queries, *new_kv, lfn, *lams]
    for c in [lf_cache] + list(kv_caches):
        in_specs += [page(c.shape[1], t) for t in range(pg)]
        args += [c] * pg
    state = pltpu.VMEM((SUBLANES, LANES), F32)
    acc = pltpu.VMEM((SUBLANES, BRANCH_W), F32)
    scratch = [state, state, acc, state, pltpu.VMEM((SUBLANES, pg * PAGE_SIZE), F32), state, state, acc, state, acc]
    out = jax.ShapeDtypeStruct((db, 1, BRANCH_W), F32)
    return _call(functools.partial(_decode_kernel, pg, lam_init), grid=(db, steps), in_specs=in_specs,
                 out_specs=[row] * 3, out_shape=[out] * 3, scratch=scratch, prefetch=1,
                 name="decode_attn")(page_table, *args)


def _layer_params(l, g_mix_pre, g_mix_post, g_ffn_pre, g_ffn_post, g_mem, w_in, b_fox_f, g_diff, w_mem_k, w_mem_v,
                  w_branch, w_gate, b_gate, w_out, w_ffn_in, conv_w, conv_b, w_ffn_out):
    row = lambda a: a[l][None, :].astype(F32)
    wt = jnp.transpose(w_in[l]).astype(BF16)
    grp = lambda g: wt[g * BRANCH_W:(g + 1) * BRANCH_W]
    wft = jnp.zeros((SUBLANES, wt.shape[1]), BF16).at[:N_HEADS].set(wt[10 * BRANCH_W:])
    bf = jnp.zeros((SUBLANES, 1), F32).at[:N_HEADS, 0].set(b_fox_f[l])
    head = jnp.arange(BRANCH_W) // HEAD_DIM
    w_main = w_in[l][:, :10 * BRANCH_W].astype(BF16)
    cols = lambda g: w_main[:, g * BRANCH_W:(g + 1) * BRANCH_W]
    return dict(
        g_mix_pre=row(g_mix_pre), g_mix_post=row(g_mix_post), g_ffn_pre=row(g_ffn_pre), g_ffn_post=row(g_ffn_post),
        g_mem=row(g_mem), g_diff=jnp.tile(g_diff[l], N_HEADS)[None, :].astype(F32),
        wmq=cols(9),
        wt=jnp.concatenate([grp(g) for g in (0, 3, 6, 1, 2, 4, 5, 7, 8)], axis=0),
        w_all=jnp.concatenate([cols(g) for g in (0, 3, 6, 9, 1, 2, 4, 5, 7, 8)], axis=1),
        wft=wft, bf=bf,
        wmkt=jnp.transpose(w_mem_k[l]).astype(BF16), wmvt=jnp.transpose(w_mem_v[l]).astype(BF16),
        hsum=(head[:, None] == head[None, :]).astype(BF16),
        wb=w_branch[l].astype(BF16), wg=w_gate[l].astype(BF16), bg=row(b_gate), wo=w_out[l].astype(BF16),
        wfi=w_ffn_in[l].astype(BF16), cw=conv_w[l].astype(F32), cb=row(conv_b), wfo=w_ffn_out[l].astype(BF16))


def _feature_major_pages(cache):
    d, n, ps, h, e = cache.shape
    return jnp.transpose(cache, (0, 1, 3, 4, 2)).reshape(d * n, h * e, ps)


def _kv_out(per_layer):
    a = jnp.stack(per_layer)
    d, b, _, s = a.shape
    return jnp.transpose(a.reshape(d, b, N_HEADS, HEAD_DIM, s), (0, 1, 4, 2, 3))


def kernel(x_prompt, mem_prompt, x_sample, cache_fox_k, cache_fox_v, cache_fox_logf, cache_diff_k, cache_diff_v,
           cache_sb_k, cache_sb_v, cache_mem_k, cache_mem_v, state_conv, page_table, g_mix_pre, g_mix_post,
           g_ffn_pre, g_ffn_post, g_mem, w_in, b_fox_f, diff_lq1, diff_lk1, diff_lq2, diff_lk2, g_diff, w_mem_k,
           w_mem_v, w_branch, w_gate, b_gate, w_out, w_ffn_in, conv_w, conv_b, w_ffn_out):
    depth = w_in.shape[0]
    batch, seq, d_model = x_prompt.shape
    db = x_sample.shape[0]
    n_pool = cache_fox_k.shape[1]
    past_len = page_table.shape[1] * PAGE_SIZE
    n_mem = cache_mem_k.shape[2]
    assert x_sample.shape[1] == 1 and seq % ATT_BLOCK == 0
    nq = seq // ATT_BLOCK

    kv_caches = [_feature_major_pages(c) for c in (cache_fox_k, cache_fox_v, cache_diff_k, cache_diff_v,
                                                   cache_sb_k, cache_sb_v)]
    lf_cache = jnp.transpose(cache_fox_logf, (0, 1, 3, 2)).reshape(depth * n_pool, N_HEADS, PAGE_SIZE)
    mem_kt = jnp.transpose(cache_mem_k, (0, 1, 3, 4, 2)).reshape(depth * db, BRANCH_W, n_mem)
    mem_vt = jnp.transpose(cache_mem_v, (0, 1, 3, 4, 2)).reshape(depth * db, BRANCH_W, n_mem)
    later = (jnp.arange(ATT_BLOCK)[:, None] < jnp.arange(ATT_BLOCK)[None, :]).astype(BF16)

    xp = x_prompt.reshape(batch * seq, d_model)
    xs = x_sample.reshape(db, d_model)
    p_rows, s_rows = [], []
    for l in range(depth):
        lam_init = 0.8 - 0.6 * math.exp(-0.3 * l)
        p = _layer_params(l, g_mix_pre, g_mix_post, g_ffn_pre, g_ffn_post, g_mem, w_in, b_fox_f, g_diff, w_mem_k,
                          w_mem_v, w_branch, w_gate, b_gate, w_out, w_ffn_in, conv_w, conv_b, w_ffn_out)
        lams = [a[l][None, :].astype(F32) for a in (diff_lq1, diff_lk1, diff_lq2, diff_lk2)]

        mq, fqt, dqt, sqt, fkt, fvt, dkt, dvt, skt, svt, lft = _project_prompt(xp, batch, seq, p)
        e = _fox_suffix(lft)
        ecol = jnp.transpose(e[:, :N_HEADS, :], (0, 2, 1))
        erow = jnp.transpose(e.reshape(batch, SUBLANES, nq, ATT_BLOCK), (0, 2, 1, 3))
        fox_o = _prompt_attention("fox", fqt, fkt, fvt, (ecol, erow))
        diff_o = _prompt_attention("diff", dqt, dkt, dvt, tuple(lams), lam_init)
        sb_o = _prompt_attention("sb", sqt, skt, svt, (later,))
        mkt, mvt = _mem_kv(mem_prompt, p)
        mem_o = _mem_attention(mq.reshape(batch, seq, BRANCH_W), mkt, mvt, ATT_BLOCK, 0)
        flat = lambda a: a.reshape(batch * seq, BRANCH_W)
        hp = _merge(xp, flat(fox_o), flat(diff_o), flat(sb_o), flat(mem_o), p, lam_init)
        xp, conv_p = _ffn_prompt(hp, batch, seq, p)
        p_rows.append((fkt, fvt, lft[:, :N_HEADS, :], dkt, dvt, skt, svt, mkt, mvt, conv_p))

        sfq, sdq, ssq, smq, sfk, sfv, sdk, sdv, ssk, ssv, slf = _project_decode(xs, past_len, p)
        one = lambda a: a.reshape(db, 1, BRANCH_W)
        slf4 = jnp.transpose(slf[:N_HEADS, :])
        fox_s, diff_s, sb_s = _decode_attention(
            page_table, l * n_pool, (one(sfq), one(sdq), one(ssq)), (one(sfk), one(sfv), one(sdk), one(sdv)),
            slf4[:, :, None], lams, lf_cache, kv_caches, lam_init)
        smq8 = jnp.zeros((db, SUBLANES, BRANCH_W), F32).at[:, 0, :].set(smq)
        mem_s = _mem_attention(smq8, mem_kt, mem_vt, SUBLANES, l * db)[:, 0, :]
        two = lambda a: a.reshape(db, BRANCH_W)
        hs = _merge(xs, two(fox_s), two(diff_s), two(sb_s), mem_s, p, lam_init)
        xs, conv_s = _ffn_decode(hs, state_conv[l], p)
        heads = lambda a: a.reshape(db, 1, N_HEADS, HEAD_DIM)
        s_rows.append((heads(sfk), heads(sfv), slf4[:, None, :], heads(sdk), heads(sdv), heads(ssk), heads(ssv), conv_s))

    pz = list(zip(*p_rows))
    p_fk, p_fv, p_dk, p_dv, p_sk, p_sv = (_kv_out(pz[i]) for i in (0, 1, 3, 4, 5, 6))
    p_fl = jnp.transpose(jnp.stack(pz[2]), (0, 1, 3, 2))
    p_mk, p_mv = _kv_out(pz[7]), _kv_out(pz[8])
    p_cv = jnp.stack(pz[9])
    s_out = [jnp.stack(a) for a in zip(*s_rows)]
    return (xp.reshape(batch, seq, d_model), xs.reshape(db, 1, d_model), p_fk, p_fv, p_fl, p_dk, p_dv, p_sk, p_sv,
            p_mk, p_mv, p_cv, *s_out)
```

```python
import functools
import math

import jax
import jax.numpy as jnp
from jax import lax
from jax.experimental import pallas as pl
from jax.experimental.pallas import tpu as pltpu

F32 = jnp.float32
BF16 = jnp.bfloat16

HEAD_DIM = 64
N_HEADS = 4
BRANCH_W = N_HEADS * HEAD_DIM
DIFF_DC = HEAD_DIM // 2
N_BRANCH = 4
PAGE_SIZE = 128
ROPE_THETA = 500000.0
ROPE_ROT = DIFF_DC // 4
RMS_EPS = 1e-6
ATTN_SCALE = HEAD_DIM ** -0.5
DIFF_SCALE = DIFF_DC ** -0.5
NEG = -1e30
LOG2E = math.log2(math.e)

LANES = 128
SUBLANES = 8
VMEM_LIMIT_BYTES = 56 * 1024 * 1024
ATT_BLOCK = 256
FF_CHUNK = 256
PAGES_PER_STEP = 16


def _call(kernel, *, grid, in_specs, out_specs, out_shape, scratch=(), prefetch=0, name):
    spec = pltpu.PrefetchScalarGridSpec(num_scalar_prefetch=prefetch, grid=grid, in_specs=in_specs,
                                        out_specs=out_specs, scratch_shapes=list(scratch))
    params = pltpu.CompilerParams(dimension_semantics=("arbitrary",) * len(grid),
                                  vmem_limit_bytes=VMEM_LIMIT_BYTES)
    return pl.pallas_call(kernel, grid_spec=spec, out_shape=out_shape, compiler_params=params, name=name)


def _resident(shape):
    zeros = (0,) * len(shape)
    return pl.BlockSpec(shape, lambda *_: zeros, pipeline_mode=pl.Buffered(1))


def _rms(x, g):
    return x * lax.rsqrt(jnp.mean(x * x, axis=-1, keepdims=True) + RMS_EPS) * g


def _log_sigmoid(z):
    return jnp.minimum(z, 0.0) - jnp.log(1.0 + jnp.exp(-jnp.abs(z)))


def _dot(a, b):
    return jnp.dot(a, b, preferred_element_type=F32)


def _dot_nt(a, b):
    return lax.dot_general(a, b, (((1,), (1,)), ((), ())), preferred_element_type=F32)


def _split_bf16(x):
    hi = x.astype(BF16)
    lo = (x - hi.astype(F32)).astype(BF16)
    return hi, lo


def _group_mask(shape, axis, width, g):
    idx = lax.broadcasted_iota(jnp.int32, shape, axis)
    return (idx >= g * width) & (idx < (g + 1) * width)


def _set_row(state, h, row):
    return jnp.where(lax.broadcasted_iota(jnp.int32, state.shape, 0) == h, row, state)


def _rope_rows(h, c, s1, s2):
    return h * c + pltpu.roll(h, BRANCH_W - ROPE_ROT // 2, 1) * s1 + pltpu.roll(h, ROPE_ROT // 2, 1) * s2


def _rope_cols(h, c, s1, s2):
    return h * c + pltpu.roll(h, BRANCH_W - ROPE_ROT // 2, 0) * s1 + pltpu.roll(h, ROPE_ROT // 2, 0) * s2


def _proj_prompt_kernel(n_carried, x_ref, g_ref, wmq_ref, wt_ref, wft_ref, bf_ref, rct_ref, rs1t_ref, rs2t_ref, *refs):
    mq_ref, fq_ref, dq_ref, sq_ref, fk_ref, fv_ref, dk_ref, dv_ref, sk_ref, sv_ref, lf_ref = refs[n_carried:]
    xn = _rms(x_ref[...], g_ref[...]).astype(BF16)
    mq_ref[...] = _dot(xn, wmq_ref[...])
    outs = (fq_ref, dq_ref, sq_ref, fk_ref, fv_ref, dk_ref, dv_ref, sk_ref, sv_ref)
    for gi, out in enumerate(outs):
        ht = _dot_nt(wt_ref[gi * BRANCH_W:(gi + 1) * BRANCH_W, :], xn)
        if out is dq_ref or out is dk_ref:
            ht = _rope_cols(ht, rct_ref[...], rs1t_ref[...], rs2t_ref[...])
        out[0] = ht
        for later_layer in range(1, out.shape[0]):
            out[later_layer] = jnp.zeros_like(ht)
    lf_ref[0] = _log_sigmoid(_dot_nt(wft_ref[...], xn) + bf_ref[...])


def _proj_decode_kernel(x_ref, g_ref, w_ref, wft_ref, bf_ref, rc_ref, rs1_ref, rs2_ref, *outs):
    xn = _rms(x_ref[...], g_ref[...]).astype(BF16)
    for gi in range(10):
        h = _dot(xn, w_ref[:, gi * BRANCH_W:(gi + 1) * BRANCH_W])
        if gi in (1, 6):
            h = _rope_rows(h, rc_ref[...], rs1_ref[...], rs2_ref[...])
        outs[gi][...] = h
    outs[10][...] = _log_sigmoid(_dot_nt(wft_ref[...], xn) + bf_ref[...])


def _rope_tables(pos):
    half = ROPE_ROT // 2
    inv_freq = ROPE_THETA ** (-jnp.arange(half, dtype=F32) * 2.0 / ROPE_ROT)
    ang = pos.astype(F32)[:, None] * inv_freq[None, :]
    cos, sin = jnp.cos(ang), jnp.sin(ang)
    n = pos.shape[0]
    pad = jnp.zeros((n, DIFF_DC - ROPE_ROT), F32)
    c = jnp.concatenate([cos, cos, pad + 1.0], axis=1)
    s1 = jnp.concatenate([-sin, jnp.zeros_like(sin), pad], axis=1)
    s2 = jnp.concatenate([jnp.zeros_like(sin), sin, pad], axis=1)
    reps = BRANCH_W // DIFF_DC
    return tuple(jnp.tile(t, (1, reps)) for t in (c, s1, s2))


def _project_prompt(x2, batch, seq, p, layer, depth, kv_stacks):
    t = batch * seq
    tm = 512 if seq % 512 == 0 else seq
    per_seq = seq // tm
    rct, rs1t, rs2t = (a.T for a in _rope_tables(jnp.arange(seq)))
    d = x2.shape[1]
    row = lambda i: (i, 0)
    tabt = lambda i: (0, i % per_seq)
    fm = lambda i: (i // per_seq, 0, i % per_seq)
    carried = list(kv_stacks or ())
    if carried:
        kv_spec = pl.BlockSpec((None, 1, BRANCH_W, tm), lambda i: (layer, i // per_seq, 0, i % per_seq))
    else:
        assert layer == 0
        kv_spec = pl.BlockSpec((depth, None, BRANCH_W, tm), lambda i: (0, i // per_seq, 0, i % per_seq))
    in_specs = [pl.BlockSpec((tm, d), row), _resident((1, d)), _resident(p['wmq'].shape), _resident(p['wt'].shape),
                _resident(p['wft'].shape), _resident((SUBLANES, 1)),
                pl.BlockSpec((BRANCH_W, tm), tabt), pl.BlockSpec((BRANCH_W, tm), tabt), pl.BlockSpec((BRANCH_W, tm), tabt)]
    in_specs += [pl.BlockSpec(memory_space=pl.ANY)] * len(carried)
    out_specs = ([pl.BlockSpec((tm, BRANCH_W), row)] + [pl.BlockSpec((1, BRANCH_W, tm), fm)] * 3
                 + [kv_spec] * 6 + [pl.BlockSpec((1, SUBLANES, tm), fm)])
    out_shape = ([jax.ShapeDtypeStruct((t, BRANCH_W), F32)]
                 + [jax.ShapeDtypeStruct((batch, BRANCH_W, seq), F32)] * 3
                 + [jax.ShapeDtypeStruct((depth, batch, BRANCH_W, seq), F32)] * 6
                 + [jax.ShapeDtypeStruct((batch, SUBLANES, seq), F32)])
    n_fixed = len(in_specs) - len(carried)
    spec = pltpu.PrefetchScalarGridSpec(num_scalar_prefetch=0, grid=(t // tm,), in_specs=in_specs, out_specs=out_specs)
    call = pl.pallas_call(
        functools.partial(_proj_prompt_kernel, len(carried)), grid_spec=spec, out_shape=out_shape,
        input_output_aliases={n_fixed + n: 4 + n for n in range(len(carried))},
        compiler_params=pltpu.CompilerParams(dimension_semantics=("arbitrary",), vmem_limit_bytes=VMEM_LIMIT_BYTES),
        name="proj_prompt")
    return call(x2, p['g_mix_pre'], p['wmq'], p['wt'], p['wft'], p['bf'], rct, rs1t, rs2t, *carried)


def _project_decode(x2, pos, p):
    m, d = x2.shape
    rc, rs1, rs2 = _rope_tables(jnp.full((m,), pos))
    full = lambda shape: pl.BlockSpec(shape, lambda i: (0,) * len(shape))
    in_specs = [full((m, d)), full((1, d)), full(p['w_all'].shape), full(p['wft'].shape), full((SUBLANES, 1)),
                full((m, BRANCH_W)), full((m, BRANCH_W)), full((m, BRANCH_W))]
    out_specs = [full((m, BRANCH_W))] * 10 + [full((SUBLANES, m))]
    out_shape = [jax.ShapeDtypeStruct((m, BRANCH_W), F32)] * 10 + [jax.ShapeDtypeStruct((SUBLANES, m), F32)]
    return _call(_proj_decode_kernel, grid=(1,), in_specs=in_specs, out_specs=out_specs, out_shape=out_shape,
                 name="proj_decode")(x2, p['g_mix_pre'], p['w_all'], p['wft'], p['bf'], rc, rs1, rs2)


def _suffix_sum_lanes(x):
    n = x.shape[1]
    lane = lax.broadcasted_iota(jnp.int32, x.shape, 1)
    d = 1
    while d < n:
        x = x + jnp.where(lane + d < n, pltpu.roll(x, n - d, 1), 0.0)
        d *= 2
    return x


def _suffix_kernel(lf_ref, e_ref):
    lf = lf_ref[0]
    e_ref[0] = _suffix_sum_lanes(lf) - lf


def _fox_suffix(lft):
    b, r, s = lft.shape
    spec = pl.BlockSpec((1, r, s), lambda i: (i, 0, 0))
    return _call(_suffix_kernel, grid=(b,), in_specs=[spec], out_specs=spec,
                 out_shape=jax.ShapeDtypeStruct(lft.shape, F32), name="fox_suffix")(lft)


def _build_q_stack(qt_ref, qs_ref, n_groups, scale, tq):
    nq = qs_ref.shape[0]
    width = BRANCH_W // n_groups
    for i in range(nq):
        qt = qt_ref[0, :, i * tq:(i + 1) * tq] * scale
        for g in range(n_groups):
            qs_ref[i, :, g * tq:(g + 1) * tq] = jnp.where(_group_mask(qt.shape, 0, width, g), qt, 0.0).astype(BF16)


def _visit_query_blocks(stages, first, nq):
    def run(blocks, masked):
        carries = [None] * len(blocks)
        for stage in stages:
            carries = [stage(i, c, masked) for i, c in zip(blocks, carries)]

    run([first], True)
    rest = nq - 1 - first
    odd = lax.rem(rest, 2)

    @pl.when(odd == 1)
    def _():
        run([first + 1], False)

    start = first + 1 + odd

    def pair(t, c):
        run([start + 2 * t, start + 2 * t + 1], False)
        return c

    lax.fori_loop(0, lax.div(rest, 2), pair, 0)


def _softmax_step(u, m_old, l_old, shift=None):
    mu = jnp.max(u, axis=0, keepdims=True)
    if shift is not None:
        mu = mu - shift
    m_new = jnp.maximum(m_old, mu)
    p = jnp.exp2(u - (m_new if shift is None else m_new + shift))
    alpha = jnp.exp2(m_old - m_new)
    return p.astype(BF16), alpha, m_new, alpha * l_old + jnp.sum(p, axis=0, keepdims=True)


def _head_rows(h):
    return slice(h * HEAD_DIM, (h + 1) * HEAD_DIM)


def _fox_prompt_kernel(qt_ref, kt_ref, vt_ref, ecol_ref, erow_ref, o_ref, qs_ref, acc_ref, m_ref, l_ref):
    tq = tk = ATT_BLOCK
    nq = qs_ref.shape[0]
    j = pl.program_id(1)

    @pl.when(j == 0)
    def _():
        _build_q_stack(qt_ref, qs_ref, N_HEADS, ATTN_SCALE * LOG2E, tq)
        m_ref[...] = jnp.full(m_ref.shape, NEG, F32)
        l_ref[...] = jnp.zeros(l_ref.shape, F32)
        acc_ref[...] = jnp.zeros(acc_ref.shape, F32)

    k = kt_ref[0].T.astype(BF16)
    vt = vt_ref[0].astype(BF16)
    ek = [jnp.broadcast_to(ecol_ref[0, :, h:h + 1] * LOG2E, (tk, tq)) for h in range(N_HEADS)]
    causal = lax.broadcasted_iota(jnp.int32, (tk, tq), 0) <= lax.broadcasted_iota(jnp.int32, (tk, tq), 1)

    def scores(i, _, masked):
        return _dot(k, qs_ref[i])

    def update(i, s_all, masked):
        m_all, l_all, eq_all = m_ref[i], l_ref[i], erow_ref[0, i] * LOG2E
        for h in range(N_HEADS):
            u = s_all[:, h * tq:(h + 1) * tq] + ek[h]
            if masked:
                u = jnp.where(causal, u, NEG)
            p, alpha, m_new, l_new = _softmax_step(u, m_all[h:h + 1], l_all[h:h + 1], shift=eq_all[h:h + 1])
            acc_ref[i, _head_rows(h), :] = acc_ref[i, _head_rows(h), :] * alpha + _dot(vt[_head_rows(h), :], p)
            m_all, l_all = _set_row(m_all, h, m_new), _set_row(l_all, h, l_new)
        m_ref[i], l_ref[i] = m_all, l_all

    _visit_query_blocks((scores, update), j, nq)

    @pl.when(j == pl.num_programs(1) - 1)
    def _():
        for i in range(nq):
            l_all = l_ref[i]
            ot = jnp.concatenate([acc_ref[i, _head_rows(h), :] * (1.0 / l_all[h:h + 1]) for h in range(N_HEADS)], axis=0)
            o_ref[0, i * tq:(i + 1) * tq, :] = ot.T


def _diff_lambda(lq1_ref, lk1_ref, lq2_ref, lk2_ref, lam_init):
    e1 = jnp.exp(jnp.sum(lq1_ref[...] * lk1_ref[...], axis=1, keepdims=True))
    e2 = jnp.exp(jnp.sum(lq2_ref[...] * lk2_ref[...], axis=1, keepdims=True))
    return e1 - e2 + lam_init


def _diff_prompt_kernel(lam_init, qt_ref, kt_ref, vt_ref, lq1_ref, lk1_ref, lq2_ref, lk2_ref, o_ref,
                        qs_ref, acc1_ref, acc2_ref, m_ref, l_ref):
    tq = tk = ATT_BLOCK
    nq = qs_ref.shape[0]
    j = pl.program_id(1)

    @pl.when(j == 0)
    def _():
        _build_q_stack(qt_ref, qs_ref, 2 * N_HEADS, DIFF_SCALE * LOG2E, tq)
        m_ref[...] = jnp.full(m_ref.shape, NEG, F32)
        l_ref[...] = jnp.zeros(l_ref.shape, F32)
        acc1_ref[...] = jnp.zeros(acc1_ref.shape, F32)
        acc2_ref[...] = jnp.zeros(acc2_ref.shape, F32)

    k = kt_ref[0].T.astype(BF16)
    vt = vt_ref[0].astype(BF16)
    causal = lax.broadcasted_iota(jnp.int32, (tk, tq), 0) <= lax.broadcasted_iota(jnp.int32, (tk, tq), 1)

    def scores(i, _, masked):
        return _dot(k, qs_ref[i])

    def update(i, s_all, masked):
        m_all, l_all = m_ref[i], l_ref[i]
        for h in range(N_HEADS):
            ps, alphas = [], []
            for g in (2 * h, 2 * h + 1):
                u = s_all[:, g * tq:(g + 1) * tq]
                if masked:
                    u = jnp.where(causal, u, NEG)
                p, alpha, m_new, l_new = _softmax_step(u, m_all[g:g + 1], l_all[g:g + 1])
                m_all, l_all = _set_row(m_all, g, m_new), _set_row(l_all, g, l_new)
                ps.append(p)
                alphas.append(alpha)
            pv = _dot(vt[_head_rows(h), :], jnp.concatenate(ps, axis=1))
            acc1_ref[i, _head_rows(h), :] = acc1_ref[i, _head_rows(h), :] * alphas[0] + pv[:, :tq]
            acc2_ref[i, _head_rows(h), :] = acc2_ref[i, _head_rows(h), :] * alphas[1] + pv[:, tq:]
        m_ref[i], l_ref[i] = m_all, l_all

    _visit_query_blocks((scores, update), j, nq)

    @pl.when(j == pl.num_programs(1) - 1)
    def _():
        lam = _diff_lambda(lq1_ref, lk1_ref, lq2_ref, lk2_ref, lam_init)
        for i in range(nq):
            l_all = l_ref[i]
            ot = jnp.concatenate(
                [acc1_ref[i, _head_rows(h), :] * (1.0 / l_all[2 * h:2 * h + 1])
                 - lam * (acc2_ref[i, _head_rows(h), :] * (1.0 / l_all[2 * h + 1:2 * h + 2])) for h in range(N_HEADS)], axis=0)
            o_ref[0, i * tq:(i + 1) * tq, :] = ot.T


def _sb_prompt_kernel(qt_ref, kt_ref, vt_ref, later_ref, o_ref, qs_ref, acc_ref, r_ref):
    tq = tk = ATT_BLOCK
    nq = qs_ref.shape[0]
    j = pl.program_id(1)
    jb = pl.num_programs(1) - 1 - j

    @pl.when(j == 0)
    def _():
        _build_q_stack(qt_ref, qs_ref, N_HEADS, ATTN_SCALE, tq)
        r_ref[...] = jnp.zeros(r_ref.shape, F32)
        acc_ref[...] = jnp.zeros(acc_ref.shape, F32)

    k = kt_ref[0].T.astype(BF16)
    vt = vt_ref[0].astype(BF16)
    later = later_ref[...]
    strict = lax.broadcasted_iota(jnp.int32, (tk, tq), 0) < lax.broadcasted_iota(jnp.int32, (tk, tq), 1)

    def scores(i, _, masked):
        return _dot(k, qs_ref[i])

    def gates(i, z_all, masked):
        r_all = r_ref[i]
        log_beta, keeps = [], []
        for h in range(N_HEADS):
            z = z_all[:, h * tq:(h + 1) * tq]
            ls = _log_sigmoid(z)
            keep = ls - z
            if masked:
                keep = jnp.where(strict, keep, 0.0)
            keeps.append(keep.astype(BF16))
            log_beta.append(ls + r_all[h:h + 1])
            r_all = _set_row(r_all, h, r_all[h:h + 1] + jnp.sum(keep, axis=0, keepdims=True))
        r_ref[i] = r_all
        tails = _dot(later, jnp.concatenate(keeps, axis=1))
        return log_beta, tails

    def accumulate(i, carry, masked):
        log_beta, tails = carry
        for h in range(N_HEADS):
            a = jnp.exp(log_beta[h] + tails[:, h * tq:(h + 1) * tq])
            if masked:
                a = jnp.where(strict, a, 0.0)
            acc_ref[i, _head_rows(h), :] += _dot(vt[_head_rows(h), :], a.astype(BF16))

    _visit_query_blocks((scores, gates, accumulate), jb, nq)

    @pl.when(j == pl.num_programs(1) - 1)
    def _():
        for i in range(nq):
            o_ref[0, i * tq:(i + 1) * tq, :] = acc_ref[i].T


def _prompt_attention(kind, layer, qt, kt, vt, extra, lam_init=None):
    b, _, s = qt.shape
    tq = tk = ATT_BLOCK
    nq, nk = s // tq, s // tk
    kblk = (lambda bi, j: nk - 1 - j) if kind == "sb" else (lambda bi, j: j)
    whole = pl.BlockSpec((1, BRANCH_W, s), lambda bi, j: (bi, 0, 0))
    kv_spec = pl.BlockSpec((None, 1, BRANCH_W, tk), lambda bi, j: (layer, bi, 0, kblk(bi, j)))
    in_specs = [whole, kv_spec, kv_spec]
    state = pltpu.VMEM((nq, SUBLANES, tq), F32)
    acc = pltpu.VMEM((nq, BRANCH_W, tq), F32)
    if kind == "fox":
        in_specs += [pl.BlockSpec((1, tk, N_HEADS), lambda bi, j: (bi, j, 0)),
                     pl.BlockSpec((1, nq, SUBLANES, tq), lambda bi, j: (bi, 0, 0, 0))]
        kernel, groups, scratch = _fox_prompt_kernel, N_HEADS, [acc, state, state]
    elif kind == "diff":
        in_specs += [_resident((1, DIFF_DC))] * 4
        kernel, groups, scratch = functools.partial(_diff_prompt_kernel, lam_init), 2 * N_HEADS, [acc, acc, state, state]
    else:
        in_specs += [_resident((tk, tk))]
        kernel, groups, scratch = _sb_prompt_kernel, N_HEADS, [acc, state]
    scratch = [pltpu.VMEM((nq, BRANCH_W, groups * tq), BF16)] + scratch
    return _call(kernel, grid=(b, nk), in_specs=in_specs,
                 out_specs=pl.BlockSpec((1, s, BRANCH_W), lambda bi, j: (bi, 0, 0)),
                 out_shape=jax.ShapeDtypeStruct((b, s, BRANCH_W), F32), scratch=scratch,
                 name=kind + "_prompt")(qt, kt, vt, *extra)


def _mem_kv_kernel(mem_ref, g_ref, wkt_ref, wvt_ref, mk_ref, mv_ref):
    mn = _rms(mem_ref[0], g_ref[...]).astype(BF16)
    mk_ref[0] = _dot_nt(wkt_ref[...], mn)
    mv_ref[0] = _dot_nt(wvt_ref[...], mn)


def _mem_kv(mem, p):
    b, n, d = mem.shape
    out = pl.BlockSpec((1, BRANCH_W, n), lambda i: (i, 0, 0))
    return _call(_mem_kv_kernel, grid=(b,),
                 in_specs=[pl.BlockSpec((1, n, d), lambda i: (i, 0, 0)), _resident((1, d)),
                           _resident((BRANCH_W, d)), _resident((BRANCH_W, d))],
                 out_specs=[out, out], out_shape=[jax.ShapeDtypeStruct((b, BRANCH_W, n), F32)] * 2,
                 name="mem_kv")(mem, p['g_mem'], p['wmkt'], p['wmvt'])


def _per_head_lanes(cols):
    tq = cols[0].shape[0]
    lane = lax.broadcasted_iota(jnp.int32, (tq, LANES), 1)
    lo = jnp.where(lane < HEAD_DIM, cols[0], cols[1])
    hi = jnp.where(lane < HEAD_DIM, cols[2], cols[3])
    return jnp.concatenate([lo, hi], axis=1)


def _mem_attn_kernel(q_ref, kt_ref, vt_ref, o_ref):
    tq = q_ref.shape[1]
    q = q_ref[0] * ATTN_SCALE
    qs = jnp.concatenate([jnp.where(_group_mask(q.shape, 1, HEAD_DIM, h), q, 0.0) for h in range(N_HEADS)], axis=0)
    s_all = _dot(qs.astype(BF16), kt_ref[0].astype(BF16))
    vt = vt_ref[0].astype(BF16)
    vbd = jnp.concatenate([jnp.where(_group_mask(vt.shape, 0, HEAD_DIM, h), vt, jnp.zeros_like(vt))
                           for h in range(N_HEADS)], axis=1)
    ps, inv = [], []
    for h in range(N_HEADS):
        s = s_all[h * tq:(h + 1) * tq]
        p = jnp.exp(s - jnp.max(s, axis=1, keepdims=True))
        inv.append(1.0 / jnp.sum(p, axis=1, keepdims=True))
        ps.append(p.astype(BF16))
    o_ref[0] = _dot_nt(jnp.concatenate(ps, axis=1), vbd) * _per_head_lanes(inv)


def _mem_attention(q, kt, vt, tq, kv_base):
    b, s, _ = q.shape
    n = kt.shape[2]
    qspec = pl.BlockSpec((1, tq, BRANCH_W), lambda bi, i: (bi, i, 0))
    kvspec = pl.BlockSpec((1, BRANCH_W, n), lambda bi, i: (kv_base + bi, 0, 0))
    return _call(_mem_attn_kernel, grid=(b, s // tq), in_specs=[qspec, kvspec, kvspec], out_specs=qspec,
                 out_shape=jax.ShapeDtypeStruct(q.shape, F32), name="mem_attn")(q, kt, vt)


def _merge_kernel(lam_init, x_ref, fo_ref, do_ref, so_ref, mo_ref, gpre_ref, gdiff_ref, hsum_ref,
                  wb_ref, wg_ref, bg_ref, wo_ref, gpost_ref, out_ref):
    x = x_ref[...]
    d_model = x.shape[1]
    xn = _rms(x, gpre_ref[...]).astype(BF16)
    d = do_ref[...]
    hi, lo = _split_bf16(d * d)
    ms = (_dot(hi, hsum_ref[...]) + _dot(lo, hsum_ref[...])) * (1.0 / HEAD_DIM)
    dn = d * lax.rsqrt(ms + RMS_EPS) * gdiff_ref[...] * (1.0 - lam_init)
    acc = jnp.zeros((x.shape[0], d_model), F32)
    for n, o in enumerate((fo_ref[...], dn, so_ref[...], mo_ref[...])):
        proj = _dot(o.astype(BF16), wb_ref[n])
        gate = jax.nn.sigmoid(_dot(xn, wg_ref[:, n * d_model:(n + 1) * d_model]) + bg_ref[:, n * d_model:(n + 1) * d_model])
        acc = acc + gate * proj
    out_ref[...] = x + _rms(_dot(acc.astype(BF16), wo_ref[...]), gpost_ref[...])


def _merge(x2, fo, do, so, mo, p, lam_init):
    m, d = x2.shape
    tm = 256 if m % 256 == 0 else m
    row = lambda w: pl.BlockSpec((tm, w), lambda i: (i, 0))
    in_specs = [row(d)] + [row(BRANCH_W)] * 4 + [_resident((1, d)), _resident((1, BRANCH_W)), _resident((BRANCH_W, BRANCH_W)),
                                                  _resident(p['wb'].shape), _resident(p['wg'].shape), _resident(p['bg'].shape),
                                                  _resident(p['wo'].shape), _resident((1, d))]
    return _call(functools.partial(_merge_kernel, lam_init), grid=(m // tm,), in_specs=in_specs, out_specs=row(d),
                 out_shape=jax.ShapeDtypeStruct((m, d), F32), name="merge")(
        x2, fo, do, so, mo, p['g_mix_pre'], p['g_diff'], p['hsum'], p['wb'], p['wg'], p['bg'], p['wo'], p['g_mix_post'])


def _ffn_cols(c):
    return slice(c * FF_CHUNK, (c + 1) * FF_CHUNK)


def _ffn_up(hn, c, win_ref, d_ff):
    return (_dot(hn, win_ref[:, _ffn_cols(c)]),
            _dot(hn, win_ref[:, d_ff + c * FF_CHUNK:d_ff + (c + 1) * FF_CHUNK]))


def _ffn_down(a, a1, a2, u, c, cw_ref, cb_ref, wout_ref):
    cols = _ffn_cols(c)
    conv = cw_ref[0:1, cols] * a2 + cw_ref[1:2, cols] * a1 + cw_ref[2:3, cols] * a + cb_ref[:, cols]
    y = jax.nn.gelu(conv, approximate=True) * u
    return _dot(y.astype(BF16), wout_ref[cols, :])


def _ffn_prompt_kernel(per_seq, h_ref, gpre_ref, win_ref, cw_ref, cb_ref, wout_ref, gpost_ref, out_ref, conv_ref, carry_ref):
    h = h_ref[...]
    tm = h.shape[0]
    d_ff = cw_ref.shape[1]
    n_chunks = d_ff // FF_CHUNK
    hn = _rms(h, gpre_ref[...]).astype(BF16)

    @pl.when(pl.program_id(0) % per_seq == 0)
    def _():
        carry_ref[...] = jnp.zeros(carry_ref.shape, F32)

    row = lax.broadcasted_iota(jnp.int32, (tm, FF_CHUNK), 0)
    f = jnp.zeros(h.shape, F32)
    nxt = _ffn_up(hn, 0, win_ref, d_ff)
    for c in range(n_chunks):
        a, u = nxt
        if c + 1 < n_chunks:
            nxt = _ffn_up(hn, c + 1, win_ref, d_ff)
        cols = _ffn_cols(c)
        pm2, pm1 = carry_ref[0:1, cols], carry_ref[1:2, cols]
        a1 = jnp.where(row == 0, pm1, pltpu.roll(a, 1, 0))
        a2 = jnp.where(row == 0, pm2, jnp.where(row == 1, pm1, pltpu.roll(a, 2, 0)))
        f = f + _ffn_down(a, a1, a2, u, c, cw_ref, cb_ref, wout_ref)
        carry_ref[0:2, cols] = a[tm - 2:tm, :]
        conv_ref[0, :, cols] = a[tm - 2:tm, :]
    out_ref[...] = h + _rms(f, gpost_ref[...])


def _ffn_decode_kernel(h_ref, gpre_ref, win_ref, cw_ref, cb_ref, wout_ref, gpost_ref, p0_ref, p1_ref, out_ref, a_ref):
    h = h_ref[...]
    d_ff = cw_ref.shape[1]
    hn = _rms(h, gpre_ref[...]).astype(BF16)
    f = jnp.zeros(h.shape, F32)
    for c in range(d_ff // FF_CHUNK):
        cols = _ffn_cols(c)
        a, u = _ffn_up(hn, c, win_ref, d_ff)
        f = f + _ffn_down(a, p1_ref[:, cols], p0_ref[:, cols], u, c, cw_ref, cb_ref, wout_ref)
        a_ref[:, cols] = a
    out_ref[...] = h + _rms(f, gpost_ref[...])


def _ffn_weights_specs(p, d):
    return [_resident((1, d)), _resident(p['wfi'].shape), _resident(p['cw'].shape), _resident(p['cb'].shape),
            _resident(p['wfo'].shape), _resident((1, d))]


def _ffn_prompt(h2, batch, seq, p):
    m, d = h2.shape
    d_ff = p['cw'].shape[1]
    tm = 256 if seq % 256 == 0 else seq
    per_seq = seq // tm
    row = pl.BlockSpec((tm, d), lambda i: (i, 0))
    return _call(functools.partial(_ffn_prompt_kernel, per_seq), grid=(m // tm,),
                 in_specs=[row] + _ffn_weights_specs(p, d),
                 out_specs=[row, pl.BlockSpec((1, 2, d_ff), lambda i: (i // per_seq, 0, 0))],
                 out_shape=[jax.ShapeDtypeStruct((m, d), F32), jax.ShapeDtypeStruct((batch, 2, d_ff), F32)],
                 scratch=[pltpu.VMEM((SUBLANES, d_ff), F32)], name="ffn_prompt")(
        h2, p['g_ffn_pre'], p['wfi'], p['cw'], p['cb'], p['wfo'], p['g_ffn_post'])


def _ffn_decode(h2, prev, p):
    m, d = h2.shape
    d_ff = p['cw'].shape[1]
    full = lambda shape: pl.BlockSpec(shape, lambda i: (0,) * len(shape))
    y, a = _call(_ffn_decode_kernel, grid=(1,),
                 in_specs=[full((m, d))] + _ffn_weights_specs(p, d) + [full((m, d_ff)), full((m, d_ff))],
                 out_specs=[full((m, d)), full((m, d_ff))],
                 out_shape=[jax.ShapeDtypeStruct((m, d), F32), jax.ShapeDtypeStruct((m, d_ff), F32)],
                 name="ffn_decode")(h2, p['g_ffn_pre'], p['wfi'], p['cw'], p['cb'], p['wfo'], p['g_ffn_post'],
                                    prev[:, 0], prev[:, 1])
    return y, jnp.stack([prev[:, 1], a], axis=1)


def _row_query(q_ref, n_groups, scale):
    shape = (SUBLANES, BRANCH_W)
    width = BRANCH_W // n_groups
    own = lax.broadcasted_iota(jnp.int32, shape, 1) // width == lax.broadcasted_iota(jnp.int32, shape, 0)
    return jnp.where(own, jnp.broadcast_to(q_ref[...] * scale, shape), 0.0), own


def _gather_pages(kv_buf, slot, c, pg):
    return jnp.concatenate([kv_buf[slot, c, t] for t in range(pg)], axis=1).astype(BF16)


def _decode_softmax_step(s, vt, m_ref, l_ref, acc_ref):
    m_old = m_ref[:, 0:1]
    m_new = jnp.maximum(m_old, jnp.max(s, axis=1, keepdims=True))
    p = jnp.exp(s - m_new)
    alpha = jnp.exp(m_old - m_new)
    l_ref[:, 0:1] = alpha * l_ref[:, 0:1] + jnp.sum(p, axis=1, keepdims=True)
    m_ref[:, 0:1] = m_new
    acc_ref[...] = alpha * acc_ref[...] + _dot_nt(p.astype(BF16), vt)


def _decode_self_init(q8, kn_ref, vn_ref, m_ref, l_ref, acc_ref):
    m_ref[:, 0:1] = jnp.sum(q8 * kn_ref[...], axis=1, keepdims=True)
    l_ref[:, 0:1] = jnp.ones((SUBLANES, 1), F32)
    acc_ref[...] = jnp.broadcast_to(vn_ref[...], acc_ref.shape)


def _decode_kernel(pg, page_base, lam_init, pt_ref, fq_ref, dq_ref, sq_ref, fkn_ref, fvn_ref, dkn_ref, dvn_ref, lfn_ref,
                   lq1_ref, lk1_ref, lq2_ref, lk2_ref, lf_hbm, fk_hbm, fv_hbm, dk_hbm, dv_hbm, sk_hbm, sv_hbm,
                   fo_ref, do_ref, so_ref, fm_ref, fl_ref, facc_ref, fc_ref, lf_ref, dm_ref, dl_ref, dacc_ref,
                   sr_ref, sacc_ref, kv_buf, lf_buf, sem):
    steps = pl.num_programs(1)
    j = pl.program_id(1)
    first, last = j == 0, j == steps - 1
    n = pl.program_id(0) * steps + j
    slot = lax.rem(n, 2)

    def page_copies(step, into):
        b, group = lax.div(step, steps), steps - 1 - lax.rem(step, steps)
        copies = []
        for t in range(pg):
            page = page_base + pt_ref[b, group * pg + t]
            copies.append(pltpu.make_async_copy(lf_hbm.at[page], lf_buf.at[into, t], sem.at[into]))
            for c, hbm in enumerate((fk_hbm, fv_hbm, dk_hbm, dv_hbm, sk_hbm, sv_hbm)):
                copies.append(pltpu.make_async_copy(hbm.at[page], kv_buf.at[into, c, t], sem.at[into]))
        return copies

    @pl.when(n == 0)
    def _():
        for copy in page_copies(n, slot):
            copy.start()

    @pl.when(n + 1 < pl.num_programs(0) * steps)
    def _():
        for copy in page_copies(n + 1, 1 - slot):
            copy.start()

    for copy in page_copies(n, slot):
        copy.wait()

    fq8, f_own = _row_query(fq_ref, N_HEADS, ATTN_SCALE)
    dq8, _ = _row_query(dq_ref, 2 * N_HEADS, DIFF_SCALE)
    sq8, s_own = _row_query(sq_ref, N_HEADS, ATTN_SCALE)

    @pl.when(first)
    def _():
        _decode_self_init(fq8, fkn_ref, fvn_ref, fm_ref, fl_ref, facc_ref)
        _decode_self_init(dq8, dkn_ref, dvn_ref, dm_ref, dl_ref, dacc_ref)
        fc_ref[...] = jnp.zeros(fc_ref.shape, F32)
        fc_ref[0:N_HEADS, 0:1] = lfn_ref[...]
        lf_ref[...] = jnp.zeros(lf_ref.shape, F32)
        sr_ref[...] = jnp.zeros(sr_ref.shape, F32)
        sacc_ref[...] = jnp.zeros(sacc_ref.shape, F32)

    pages = functools.partial(_gather_pages, kv_buf, slot, pg=pg)
    for t in range(pg):
        lf_ref[0:N_HEADS, t * PAGE_SIZE:(t + 1) * PAGE_SIZE] = lf_buf[slot, t]
    lf = lf_ref[...]
    incl = _suffix_sum_lanes(lf)
    carry = fc_ref[:, 0:1]
    fc_ref[:, 0:1] = carry + incl[:, 0:1]
    s = _dot(fq8.astype(BF16), pages(c=0)) + (incl - lf + carry)
    _decode_softmax_step(s, pages(c=1), fm_ref, fl_ref, facc_ref)

    s = _dot(dq8.astype(BF16), pages(c=2))
    _decode_softmax_step(s, pages(c=3), dm_ref, dl_ref, dacc_ref)

    z = _dot(sq8.astype(BF16), pages(c=4))
    ls = _log_sigmoid(z)
    keep = ls - z
    incl = _suffix_sum_lanes(keep)
    r_old = sr_ref[:, 0:1]
    a = jnp.exp(ls + (incl - keep) + r_old)
    sr_ref[:, 0:1] = r_old + incl[:, 0:1]
    sacc_ref[...] += _dot_nt(a.astype(BF16), pages(c=5))

    @pl.when(last)
    def _():
        fo_ref[...] = jnp.sum(jnp.where(f_own, facc_ref[...] / fl_ref[:, 0:1], 0.0), axis=0, keepdims=True)
        lam = _diff_lambda(lq1_ref, lk1_ref, lq2_ref, lk2_ref, lam_init)
        shape = (SUBLANES, BRANCH_W)
        row = lax.broadcasted_iota(jnp.int32, shape, 0)
        head = lax.broadcasted_iota(jnp.int32, shape, 1) // HEAD_DIM
        o8 = dacc_ref[...] / dl_ref[:, 0:1] * jnp.where(row % 2 == 0, 1.0, -lam)
        do_ref[...] = jnp.sum(jnp.where(head == row // 2, o8, 0.0), axis=0, keepdims=True)
        so_ref[...] = jnp.sum(jnp.where(s_own, sacc_ref[...], 0.0), axis=0, keepdims=True)


def _decode_attention(page_table, page_base, queries, new_kv, lfn, lams, lf_cache, kv_caches, lam_init):
    db, n_pages = page_table.shape
    pg = PAGES_PER_STEP if n_pages % PAGES_PER_STEP == 0 else n_pages
    steps = n_pages // pg

    row = pl.BlockSpec((None, 1, BRANCH_W), lambda b, j, pt: (b, 0, 0))
    small = lambda shape: pl.BlockSpec(shape, lambda b, j, pt: (0,) * len(shape))
    in_hbm = pl.BlockSpec(memory_space=pl.ANY)
    in_specs = ([row] * 7 + [pl.BlockSpec((None, N_HEADS, 1), lambda b, j, pt: (b, 0, 0))] + [small((1, DIFF_DC))] * 4
                + [in_hbm] * 7)
    state = pltpu.VMEM((SUBLANES, LANES), F32)
    acc = pltpu.VMEM((SUBLANES, BRANCH_W), F32)
    scratch = [state, state, acc, state, pltpu.VMEM((SUBLANES, pg * PAGE_SIZE), F32), state, state, acc, state, acc,
               pltpu.VMEM((2, len(kv_caches), pg, BRANCH_W, PAGE_SIZE), F32),
               pltpu.VMEM((2, pg, N_HEADS, PAGE_SIZE), F32),
               pltpu.SemaphoreType.DMA((2,))]
    out = jax.ShapeDtypeStruct((db, 1, BRANCH_W), F32)
    return _call(functools.partial(_decode_kernel, pg, page_base, lam_init), grid=(db, steps), in_specs=in_specs,
                 out_specs=[row] * 3, out_shape=[out] * 3, scratch=scratch, prefetch=1,
                 name="decode_attn")(page_table, *queries, *new_kv, lfn, *lams, lf_cache, *kv_caches)


def _layer_params(l, g_mix_pre, g_mix_post, g_ffn_pre, g_ffn_post, g_mem, w_in, b_fox_f, g_diff, w_mem_k, w_mem_v,
                  w_branch, w_gate, b_gate, w_out, w_ffn_in, conv_w, conv_b, w_ffn_out):
    row = lambda a: a[l][None, :].astype(F32)
    wt = jnp.transpose(w_in[l]).astype(BF16)
    grp = lambda g: wt[g * BRANCH_W:(g + 1) * BRANCH_W]
    wft = jnp.zeros((SUBLANES, wt.shape[1]), BF16).at[:N_HEADS].set(wt[10 * BRANCH_W:])
    bf = jnp.zeros((SUBLANES, 1), F32).at[:N_HEADS, 0].set(b_fox_f[l])
    head = jnp.arange(BRANCH_W) // HEAD_DIM
    w_main = w_in[l][:, :10 * BRANCH_W].astype(BF16)
    cols = lambda g: w_main[:, g * BRANCH_W:(g + 1) * BRANCH_W]
    return dict(
        g_mix_pre=row(g_mix_pre), g_mix_post=row(g_mix_post), g_ffn_pre=row(g_ffn_pre), g_ffn_post=row(g_ffn_post),
        g_mem=row(g_mem), g_diff=jnp.tile(g_diff[l], N_HEADS)[None, :].astype(F32),
        wmq=cols(9),
        wt=jnp.concatenate([grp(g) for g in (0, 3, 6, 1, 2, 4, 5, 7, 8)], axis=0),
        w_all=jnp.concatenate([cols(g) for g in (0, 3, 6, 9, 1, 2, 4, 5, 7, 8)], axis=1),
        wft=wft, bf=bf,
        wmkt=jnp.transpose(w_mem_k[l]).astype(BF16), wmvt=jnp.transpose(w_mem_v[l]).astype(BF16),
        hsum=(head[:, None] == head[None, :]).astype(BF16),
        wb=w_branch[l].astype(BF16), wg=w_gate[l].astype(BF16), bg=row(b_gate), wo=w_out[l].astype(BF16),
        wfi=w_ffn_in[l].astype(BF16), cw=conv_w[l].astype(F32), cb=row(conv_b), wfo=w_ffn_out[l].astype(BF16))


def _feature_major_pages(cache):
    d, n, ps, h, e = cache.shape
    return jnp.transpose(cache, (0, 1, 3, 4, 2)).reshape(d * n, h * e, ps)


def _kv_out(a):
    d, b, _, s = a.shape
    return jnp.transpose(a.reshape(d, b, N_HEADS, HEAD_DIM, s), (0, 1, 4, 2, 3))


def kernel(x_prompt, mem_prompt, x_sample, cache_fox_k, cache_fox_v, cache_fox_logf, cache_diff_k, cache_diff_v,
           cache_sb_k, cache_sb_v, cache_mem_k, cache_mem_v, state_conv, page_table, g_mix_pre, g_mix_post,
           g_ffn_pre, g_ffn_post, g_mem, w_in, b_fox_f, diff_lq1, diff_lk1, diff_lq2, diff_lk2, g_diff, w_mem_k,
           w_mem_v, w_branch, w_gate, b_gate, w_out, w_ffn_in, conv_w, conv_b, w_ffn_out):
    depth = w_in.shape[0]
    batch, seq, d_model = x_prompt.shape
    db = x_sample.shape[0]
    n_pool = cache_fox_k.shape[1]
    past_len = page_table.shape[1] * PAGE_SIZE
    n_mem = cache_mem_k.shape[2]
    assert x_sample.shape[1] == 1 and seq % ATT_BLOCK == 0
    nq = seq // ATT_BLOCK

    kv_caches = [_feature_major_pages(c) for c in (cache_fox_k, cache_fox_v, cache_diff_k, cache_diff_v,
                                                   cache_sb_k, cache_sb_v)]
    lf_cache = jnp.transpose(cache_fox_logf, (0, 1, 3, 2)).reshape(depth * n_pool, N_HEADS, PAGE_SIZE)
    mem_kt = jnp.transpose(cache_mem_k, (0, 1, 3, 4, 2)).reshape(depth * db, BRANCH_W, n_mem)
    mem_vt = jnp.transpose(cache_mem_v, (0, 1, 3, 4, 2)).reshape(depth * db, BRANCH_W, n_mem)
    later = (jnp.arange(ATT_BLOCK)[:, None] < jnp.arange(ATT_BLOCK)[None, :]).astype(BF16)

    xp = x_prompt.reshape(batch * seq, d_model)
    xs = x_sample.reshape(db, d_model)
    p_rows, s_rows, kv_stacks = [], [], None
    for l in range(depth):
        lam_init = 0.8 - 0.6 * math.exp(-0.3 * l)
        p = _layer_params(l, g_mix_pre, g_mix_post, g_ffn_pre, g_ffn_post, g_mem, w_in, b_fox_f, g_diff, w_mem_k,
                          w_mem_v, w_branch, w_gate, b_gate, w_out, w_ffn_in, conv_w, conv_b, w_ffn_out)
        lams = [a[l][None, :].astype(F32) for a in (diff_lq1, diff_lk1, diff_lq2, diff_lk2)]

        mq, fqt, dqt, sqt, *kv_stacks, lft = _project_prompt(xp, batch, seq, p, l, depth, kv_stacks)
        fkt, fvt, dkt, dvt, skt, svt = kv_stacks
        e = _fox_suffix(lft)
        ecol = jnp.transpose(e[:, :N_HEADS, :], (0, 2, 1))
        erow = jnp.transpose(e.reshape(batch, SUBLANES, nq, ATT_BLOCK), (0, 2, 1, 3))
        fox_o = _prompt_attention("fox", l, fqt, fkt, fvt, (ecol, erow))
        diff_o = _prompt_attention("diff", l, dqt, dkt, dvt, tuple(lams), lam_init)
        sb_o = _prompt_attention("sb", l, sqt, skt, svt, (later,))
        mkt, mvt = _mem_kv(mem_prompt, p)
        mem_o = _mem_attention(mq.reshape(batch, seq, BRANCH_W), mkt, mvt, ATT_BLOCK, 0)
        flat = lambda a: a.reshape(batch * seq, BRANCH_W)
        hp = _merge(xp, flat(fox_o), flat(diff_o), flat(sb_o), flat(mem_o), p, lam_init)
        xp, conv_p = _ffn_prompt(hp, batch, seq, p)
        p_rows.append((lft[:, :N_HEADS, :], mkt, mvt, conv_p))

        sfq, sdq, ssq, smq, sfk, sfv, sdk, sdv, ssk, ssv, slf = _project_decode(xs, past_len, p)
        one = lambda a: a.reshape(db, 1, BRANCH_W)
        slf4 = jnp.transpose(slf[:N_HEADS, :])
        fox_s, diff_s, sb_s = _decode_attention(
            page_table, l * n_pool, (one(sfq), one(sdq), one(ssq)), (one(sfk), one(sfv), one(sdk), one(sdv)),
            slf4[:, :, None], lams, lf_cache, kv_caches, lam_init)
        smq8 = jnp.zeros((db, SUBLANES, BRANCH_W), F32).at[:, 0, :].set(smq)
        mem_s = _mem_attention(smq8, mem_kt, mem_vt, SUBLANES, l * db)[:, 0, :]
        two = lambda a: a.reshape(db, BRANCH_W)
        hs = _merge(xs, two(fox_s), two(diff_s), two(sb_s), mem_s, p, lam_init)
        xs, conv_s = _ffn_decode(hs, state_conv[l], p)
        heads = lambda a: a.reshape(db, 1, N_HEADS, HEAD_DIM)
        s_rows.append((heads(sfk), heads(sfv), slf4[:, None, :], heads(sdk), heads(sdv), heads(ssk), heads(ssv), conv_s))

    p_fk, p_fv, p_dk, p_dv, p_sk, p_sv = (_kv_out(a) for a in kv_stacks)
    p_fl, p_mk, p_mv, p_cv = (jnp.stack(a) for a in zip(*p_rows))
    p_fl = jnp.transpose(p_fl, (0, 1, 3, 2))
    p_mk, p_mv = _kv_out(p_mk), _kv_out(p_mv)
    s_out = [jnp.stack(a) for a in zip(*s_rows)]
    return (xp.reshape(batch, seq, d_model), xs.reshape(db, 1, d_model), p_fk, p_fv, p_fl, p_dk, p_dv, p_sk, p_sv,
            p_mk, p_mv, p_cv, *s_out)
```

```python
import functools
import math

import jax
import jax.numpy as jnp
from jax import lax
from jax.experimental import pallas as pl
from jax.experimental.pallas import tpu as pltpu

F32 = jnp.float32
BF16 = jnp.bfloat16

HEAD_DIM = 64
N_HEADS = 4
BRANCH_W = N_HEADS * HEAD_DIM
DIFF_DC = HEAD_DIM // 2
N_BRANCH = 4
PAGE_SIZE = 128
ROPE_THETA = 500000.0
ROPE_ROT = DIFF_DC // 4
RMS_EPS = 1e-6
ATTN_SCALE = HEAD_DIM ** -0.5
DIFF_SCALE = DIFF_DC ** -0.5
NEG = -1e30
LOG2E = math.log2(math.e)

LANES = 128
SUBLANES = 8
VMEM_LIMIT_BYTES = 56 * 1024 * 1024
ATT_BLOCK = 256
FF_CHUNK = 256
PAGES_PER_STEP = 16


def _call(kernel, *, grid, in_specs, out_specs, out_shape, scratch=(), prefetch=0, name):
    spec = pltpu.PrefetchScalarGridSpec(num_scalar_prefetch=prefetch, grid=grid, in_specs=in_specs,
                                        out_specs=out_specs, scratch_shapes=list(scratch))
    params = pltpu.CompilerParams(dimension_semantics=("arbitrary",) * len(grid),
                                  vmem_limit_bytes=VMEM_LIMIT_BYTES)
    return pl.pallas_call(kernel, grid_spec=spec, out_shape=out_shape, compiler_params=params, name=name)


def _resident(shape):
    zeros = (0,) * len(shape)
    return pl.BlockSpec(shape, lambda *_: zeros, pipeline_mode=pl.Buffered(1))


def _rms(x, g):
    return x * lax.rsqrt(jnp.mean(x * x, axis=-1, keepdims=True) + RMS_EPS) * g


def _log_sigmoid(z):
    return jnp.minimum(z, 0.0) - jnp.log(1.0 + jnp.exp(-jnp.abs(z)))


def _dot(a, b):
    return jnp.dot(a, b, preferred_element_type=F32)


def _dot_nt(a, b):
    return lax.dot_general(a, b, (((1,), (1,)), ((), ())), preferred_element_type=F32)


def _split_bf16(x):
    hi = x.astype(BF16)
    lo = (x - hi.astype(F32)).astype(BF16)
    return hi, lo


def _group_mask(shape, axis, width, g):
    idx = lax.broadcasted_iota(jnp.int32, shape, axis)
    return (idx >= g * width) & (idx < (g + 1) * width)


def _set_row(state, h, row):
    return jnp.where(lax.broadcasted_iota(jnp.int32, state.shape, 0) == h, row, state)


def _rope_rows(h, c, s1, s2):
    return h * c + pltpu.roll(h, BRANCH_W - ROPE_ROT // 2, 1) * s1 + pltpu.roll(h, ROPE_ROT // 2, 1) * s2


def _rope_cols(h, c, s1, s2):
    return h * c + pltpu.roll(h, BRANCH_W - ROPE_ROT // 2, 0) * s1 + pltpu.roll(h, ROPE_ROT // 2, 0) * s2


def _proj_prompt_kernel(n_carried, x_ref, g_ref, wt_ref, wft_ref, bf_ref, rct_ref, rs1t_ref, rs2t_ref, *refs):
    fq_ref, dq_ref, sq_ref, mq_ref, fk_ref, fv_ref, dk_ref, dv_ref, sk_ref, sv_ref, lf_ref = refs[n_carried:]
    xn = _rms(x_ref[...], g_ref[...]).astype(BF16)
    outs = (fq_ref, dq_ref, sq_ref, mq_ref, fk_ref, fv_ref, dk_ref, dv_ref, sk_ref, sv_ref)
    for gi, out in enumerate(outs):
        ht = _dot_nt(wt_ref[gi * BRANCH_W:(gi + 1) * BRANCH_W, :], xn)
        if out is dq_ref or out is dk_ref:
            ht = _rope_cols(ht, rct_ref[...], rs1t_ref[...], rs2t_ref[...])
        out[0] = ht
        for later_layer in range(1, out.shape[0]):
            out[later_layer] = jnp.zeros_like(ht)
    lf_ref[0] = _log_sigmoid(_dot_nt(wft_ref[...], xn) + bf_ref[...])


def _proj_decode_kernel(x_ref, g_ref, w_ref, wft_ref, bf_ref, rc_ref, rs1_ref, rs2_ref, *outs):
    xn = _rms(x_ref[...], g_ref[...]).astype(BF16)
    for gi in range(10):
        h = _dot(xn, w_ref[:, gi * BRANCH_W:(gi + 1) * BRANCH_W])
        if gi in (1, 6):
            h = _rope_rows(h, rc_ref[...], rs1_ref[...], rs2_ref[...])
        outs[gi][...] = h
    outs[10][...] = _log_sigmoid(_dot_nt(wft_ref[...], xn) + bf_ref[...])


def _rope_tables(pos):
    half = ROPE_ROT // 2
    inv_freq = ROPE_THETA ** (-jnp.arange(half, dtype=F32) * 2.0 / ROPE_ROT)
    ang = pos.astype(F32)[:, None] * inv_freq[None, :]
    cos, sin = jnp.cos(ang), jnp.sin(ang)
    n = pos.shape[0]
    pad = jnp.zeros((n, DIFF_DC - ROPE_ROT), F32)
    c = jnp.concatenate([cos, cos, pad + 1.0], axis=1)
    s1 = jnp.concatenate([-sin, jnp.zeros_like(sin), pad], axis=1)
    s2 = jnp.concatenate([jnp.zeros_like(sin), sin, pad], axis=1)
    reps = BRANCH_W // DIFF_DC
    return tuple(jnp.tile(t, (1, reps)) for t in (c, s1, s2))


def _project_prompt(x2, batch, seq, p, layer, depth, kv_stacks):
    t = batch * seq
    tm = 512 if seq % 512 == 0 else seq
    per_seq = seq // tm
    rct, rs1t, rs2t = (a.T for a in _rope_tables(jnp.arange(seq)))
    d = x2.shape[1]
    row = lambda i: (i, 0)
    tabt = lambda i: (0, i % per_seq)
    fm = lambda i: (i // per_seq, 0, i % per_seq)
    carried = list(kv_stacks or ())
    if carried:
        kv_spec = pl.BlockSpec((None, 1, BRANCH_W, tm), lambda i: (layer, i // per_seq, 0, i % per_seq))
    else:
        assert layer == 0
        kv_spec = pl.BlockSpec((depth, None, BRANCH_W, tm), lambda i: (0, i // per_seq, 0, i % per_seq))
    in_specs = [pl.BlockSpec((tm, d), row), _resident((1, d)), _resident(p['wt'].shape),
                _resident(p['wft'].shape), _resident((SUBLANES, 1)),
                pl.BlockSpec((BRANCH_W, tm), tabt), pl.BlockSpec((BRANCH_W, tm), tabt), pl.BlockSpec((BRANCH_W, tm), tabt)]
    in_specs += [pl.BlockSpec(memory_space=pl.ANY)] * len(carried)
    out_specs = ([pl.BlockSpec((1, BRANCH_W, tm), fm)] * 4 + [kv_spec] * 6 + [pl.BlockSpec((1, SUBLANES, tm), fm)])
    out_shape = ([jax.ShapeDtypeStruct((batch, BRANCH_W, seq), F32)] * 4
                 + [jax.ShapeDtypeStruct((depth, batch, BRANCH_W, seq), F32)] * 6
                 + [jax.ShapeDtypeStruct((batch, SUBLANES, seq), F32)])
    n_fixed = len(in_specs) - len(carried)
    spec = pltpu.PrefetchScalarGridSpec(num_scalar_prefetch=0, grid=(t // tm,), in_specs=in_specs, out_specs=out_specs)
    call = pl.pallas_call(
        functools.partial(_proj_prompt_kernel, len(carried)), grid_spec=spec, out_shape=out_shape,
        input_output_aliases={n_fixed + n: 4 + n for n in range(len(carried))},
        compiler_params=pltpu.CompilerParams(dimension_semantics=("arbitrary",), vmem_limit_bytes=VMEM_LIMIT_BYTES),
        name="proj_prompt")
    return call(x2, p['g_mix_pre'], p['wt'], p['wft'], p['bf'], rct, rs1t, rs2t, *carried)


def _project_decode(x2, pos, p):
    m, d = x2.shape
    rc, rs1, rs2 = _rope_tables(jnp.full((m,), pos))
    full = lambda shape: pl.BlockSpec(shape, lambda i: (0,) * len(shape))
    in_specs = [full((m, d)), full((1, d)), full(p['w_all'].shape), full(p['wft'].shape), full((SUBLANES, 1)),
                full((m, BRANCH_W)), full((m, BRANCH_W)), full((m, BRANCH_W))]
    out_specs = [full((m, BRANCH_W))] * 10 + [full((SUBLANES, m))]
    out_shape = [jax.ShapeDtypeStruct((m, BRANCH_W), F32)] * 10 + [jax.ShapeDtypeStruct((SUBLANES, m), F32)]
    return _call(_proj_decode_kernel, grid=(1,), in_specs=in_specs, out_specs=out_specs, out_shape=out_shape,
                 name="proj_decode")(x2, p['g_mix_pre'], p['w_all'], p['wft'], p['bf'], rc, rs1, rs2)


def _suffix_sum_lanes(x):
    n = x.shape[1]
    lane = lax.broadcasted_iota(jnp.int32, x.shape, 1)
    d = 1
    while d < n:
        x = x + jnp.where(lane + d < n, pltpu.roll(x, n - d, 1), 0.0)
        d *= 2
    return x


def _suffix_kernel(lf_ref, e_ref):
    lf = lf_ref[0]
    e_ref[0] = _suffix_sum_lanes(lf) - lf


def _fox_suffix(lft):
    b, r, s = lft.shape
    spec = pl.BlockSpec((1, r, s), lambda i: (i, 0, 0))
    return _call(_suffix_kernel, grid=(b,), in_specs=[spec], out_specs=spec,
                 out_shape=jax.ShapeDtypeStruct(lft.shape, F32), name="fox_suffix")(lft)


def _build_q_stack(qt_ref, qs_ref, n_groups, scale, tq):
    nq = qs_ref.shape[0]
    width = BRANCH_W // n_groups
    for i in range(nq):
        qt = qt_ref[0, :, i * tq:(i + 1) * tq] * scale
        for g in range(n_groups):
            qs_ref[i, :, g * tq:(g + 1) * tq] = jnp.where(_group_mask(qt.shape, 0, width, g), qt, 0.0).astype(BF16)


def _visit_query_blocks(stages, first, nq):
    def run(blocks, masked):
        carries = [None] * len(blocks)
        for stage in stages:
            carries = [stage(i, c, masked) for i, c in zip(blocks, carries)]

    run([first], True)
    rest = nq - 1 - first
    odd = lax.rem(rest, 2)

    @pl.when(odd == 1)
    def _():
        run([first + 1], False)

    start = first + 1 + odd

    def pair(t, c):
        run([start + 2 * t, start + 2 * t + 1], False)
        return c

    lax.fori_loop(0, lax.div(rest, 2), pair, 0)


def _softmax_step(u, m_old, l_old, shift=None):
    mu = jnp.max(u, axis=0, keepdims=True)
    if shift is not None:
        mu = mu - shift
    m_new = jnp.maximum(m_old, mu)
    p = jnp.exp2(u - (m_new if shift is None else m_new + shift))
    alpha = jnp.exp2(m_old - m_new)
    return p.astype(BF16), alpha, m_new, alpha * l_old + jnp.sum(p, axis=0, keepdims=True)


def _head_rows(h):
    return slice(h * HEAD_DIM, (h + 1) * HEAD_DIM)


def _fox_prompt_kernel(qt_ref, kt_ref, vt_ref, ecol_ref, erow_ref, o_ref, qs_ref, acc_ref, m_ref, l_ref):
    tq = tk = ATT_BLOCK
    nq = qs_ref.shape[0]
    j = pl.program_id(1)

    @pl.when(j == 0)
    def _():
        _build_q_stack(qt_ref, qs_ref, N_HEADS, ATTN_SCALE * LOG2E, tq)
        m_ref[...] = jnp.full(m_ref.shape, NEG, F32)
        l_ref[...] = jnp.zeros(l_ref.shape, F32)
        acc_ref[...] = jnp.zeros(acc_ref.shape, F32)

    k = kt_ref[0].T.astype(BF16)
    vt = vt_ref[0].astype(BF16)
    ek = [jnp.broadcast_to(ecol_ref[0, :, h:h + 1] * LOG2E, (tk, tq)) for h in range(N_HEADS)]
    causal = lax.broadcasted_iota(jnp.int32, (tk, tq), 0) <= lax.broadcasted_iota(jnp.int32, (tk, tq), 1)

    def scores(i, _, masked):
        return _dot(k, qs_ref[i])

    def update(i, s_all, masked):
        m_all, l_all, eq_all = m_ref[i], l_ref[i], erow_ref[0, i] * LOG2E
        for h in range(N_HEADS):
            u = s_all[:, h * tq:(h + 1) * tq] + ek[h]
            if masked:
                u = jnp.where(causal, u, NEG)
            p, alpha, m_new, l_new = _softmax_step(u, m_all[h:h + 1], l_all[h:h + 1], shift=eq_all[h:h + 1])
            acc_ref[i, _head_rows(h), :] = acc_ref[i, _head_rows(h), :] * alpha + _dot(vt[_head_rows(h), :], p)
            m_all, l_all = _set_row(m_all, h, m_new), _set_row(l_all, h, l_new)
        m_ref[i], l_ref[i] = m_all, l_all

    _visit_query_blocks((scores, update), j, nq)

    @pl.when(j == pl.num_programs(1) - 1)
    def _():
        for i in range(nq):
            l_all = l_ref[i]
            ot = jnp.concatenate([acc_ref[i, _head_rows(h), :] * (1.0 / l_all[h:h + 1]) for h in range(N_HEADS)], axis=0)
            o_ref[0, i * tq:(i + 1) * tq, :] = ot.T


def _diff_lambda(lq1_ref, lk1_ref, lq2_ref, lk2_ref, lam_init):
    e1 = jnp.exp(jnp.sum(lq1_ref[...] * lk1_ref[...], axis=1, keepdims=True))
    e2 = jnp.exp(jnp.sum(lq2_ref[...] * lk2_ref[...], axis=1, keepdims=True))
    return e1 - e2 + lam_init


def _diff_prompt_kernel(lam_init, qt_ref, kt_ref, vt_ref, lq1_ref, lk1_ref, lq2_ref, lk2_ref, o_ref,
                        qs_ref, acc1_ref, acc2_ref, m_ref, l_ref):
    tq = tk = ATT_BLOCK
    nq = qs_ref.shape[0]
    j = pl.program_id(1)

    @pl.when(j == 0)
    def _():
        _build_q_stack(qt_ref, qs_ref, 2 * N_HEADS, DIFF_SCALE * LOG2E, tq)
        m_ref[...] = jnp.full(m_ref.shape, NEG, F32)
        l_ref[...] = jnp.zeros(l_ref.shape, F32)
        acc1_ref[...] = jnp.zeros(acc1_ref.shape, F32)
        acc2_ref[...] = jnp.zeros(acc2_ref.shape, F32)

    k = kt_ref[0].T.astype(BF16)
    vt = vt_ref[0].astype(BF16)
    causal = lax.broadcasted_iota(jnp.int32, (tk, tq), 0) <= lax.broadcasted_iota(jnp.int32, (tk, tq), 1)

    def scores(i, _, masked):
        return _dot(k, qs_ref[i])

    def update(i, s_all, masked):
        m_all, l_all = m_ref[i], l_ref[i]
        for h in range(N_HEADS):
            ps, alphas = [], []
            for g in (2 * h, 2 * h + 1):
                u = s_all[:, g * tq:(g + 1) * tq]
                if masked:
                    u = jnp.where(causal, u, NEG)
                p, alpha, m_new, l_new = _softmax_step(u, m_all[g:g + 1], l_all[g:g + 1])
                m_all, l_all = _set_row(m_all, g, m_new), _set_row(l_all, g, l_new)
                ps.append(p)
                alphas.append(alpha)
            pv = _dot(vt[_head_rows(h), :], jnp.concatenate(ps, axis=1))
            acc1_ref[i, _head_rows(h), :] = acc1_ref[i, _head_rows(h), :] * alphas[0] + pv[:, :tq]
            acc2_ref[i, _head_rows(h), :] = acc2_ref[i, _head_rows(h), :] * alphas[1] + pv[:, tq:]
        m_ref[i], l_ref[i] = m_all, l_all

    _visit_query_blocks((scores, update), j, nq)

    @pl.when(j == pl.num_programs(1) - 1)
    def _():
        lam = _diff_lambda(lq1_ref, lk1_ref, lq2_ref, lk2_ref, lam_init)
        for i in range(nq):
            l_all = l_ref[i]
            ot = jnp.concatenate(
                [acc1_ref[i, _head_rows(h), :] * (1.0 / l_all[2 * h:2 * h + 1])
                 - lam * (acc2_ref[i, _head_rows(h), :] * (1.0 / l_all[2 * h + 1:2 * h + 2])) for h in range(N_HEADS)], axis=0)
            o_ref[0, i * tq:(i + 1) * tq, :] = ot.T


def _sb_prompt_kernel(qt_ref, kt_ref, vt_ref, later_ref, o_ref, qs_ref, acc_ref, r_ref):
    tq = tk = ATT_BLOCK
    nq = qs_ref.shape[0]
    j = pl.program_id(1)
    jb = pl.num_programs(1) - 1 - j

    @pl.when(j == 0)
    def _():
        _build_q_stack(qt_ref, qs_ref, N_HEADS, ATTN_SCALE * LOG2E, tq)
        r_ref[...] = jnp.zeros(r_ref.shape, F32)
        acc_ref[...] = jnp.zeros(acc_ref.shape, F32)

    k = kt_ref[0].T.astype(BF16)
    vt = vt_ref[0].astype(BF16)
    later = later_ref[...]
    strict = lax.broadcasted_iota(jnp.int32, (tk, tq), 0) < lax.broadcasted_iota(jnp.int32, (tk, tq), 1)

    def scores(i, _, masked):
        return _dot(k, qs_ref[i])

    def gates(i, z_all, masked):
        r_all = r_ref[i]
        log_beta, keeps = [], []
        for h in range(N_HEADS):
            z = z_all[:, h * tq:(h + 1) * tq]
            ls = jnp.minimum(z, 0.0) - jnp.log2(1.0 + jnp.exp2(-jnp.abs(z)))
            keep = ls - z
            if masked:
                keep = jnp.where(strict, keep, 0.0)
            keeps.append(keep.astype(BF16))
            log_beta.append(ls + r_all[h:h + 1])
            r_all = _set_row(r_all, h, r_all[h:h + 1] + jnp.sum(keep, axis=0, keepdims=True))
        r_ref[i] = r_all
        tails = _dot(later, jnp.concatenate(keeps, axis=1))
        return log_beta, tails

    def accumulate(i, carry, masked):
        log_beta, tails = carry
        for h in range(N_HEADS):
            a = jnp.exp2(log_beta[h] + tails[:, h * tq:(h + 1) * tq])
            if masked:
                a = jnp.where(strict, a, 0.0)
            acc_ref[i, _head_rows(h), :] += _dot(vt[_head_rows(h), :], a.astype(BF16))

    _visit_query_blocks((scores, gates, accumulate), jb, nq)

    @pl.when(j == pl.num_programs(1) - 1)
    def _():
        for i in range(nq):
            o_ref[0, i * tq:(i + 1) * tq, :] = acc_ref[i].T


def _prompt_attention(kind, layer, qt, kt, vt, extra, lam_init=None):
    b, _, s = qt.shape
    tq = tk = ATT_BLOCK
    nq, nk = s // tq, s // tk
    kblk = (lambda bi, j: nk - 1 - j) if kind == "sb" else (lambda bi, j: j)
    whole = pl.BlockSpec((1, BRANCH_W, s), lambda bi, j: (bi, 0, 0))
    kv_spec = pl.BlockSpec((None, 1, BRANCH_W, tk), lambda bi, j: (layer, bi, 0, kblk(bi, j)))
    in_specs = [whole, kv_spec, kv_spec]
    state = pltpu.VMEM((nq, SUBLANES, tq), F32)
    acc = pltpu.VMEM((nq, BRANCH_W, tq), F32)
    if kind == "fox":
        in_specs += [pl.BlockSpec((1, tk, N_HEADS), lambda bi, j: (bi, j, 0)),
                     pl.BlockSpec((1, nq, SUBLANES, tq), lambda bi, j: (bi, 0, 0, 0))]
        kernel, groups, scratch = _fox_prompt_kernel, N_HEADS, [acc, state, state]
    elif kind == "diff":
        in_specs += [_resident((1, DIFF_DC))] * 4
        kernel, groups, scratch = functools.partial(_diff_prompt_kernel, lam_init), 2 * N_HEADS, [acc, acc, state, state]
    else:
        in_specs += [_resident((tk, tk))]
        kernel, groups, scratch = _sb_prompt_kernel, N_HEADS, [acc, state]
    scratch = [pltpu.VMEM((nq, BRANCH_W, groups * tq), BF16)] + scratch
    return _call(kernel, grid=(b, nk), in_specs=in_specs,
                 out_specs=pl.BlockSpec((1, s, BRANCH_W), lambda bi, j: (bi, 0, 0)),
                 out_shape=jax.ShapeDtypeStruct((b, s, BRANCH_W), F32), scratch=scratch,
                 name=kind + "_prompt")(qt, kt, vt, *extra)


def _mem_kv_kernel(mem_ref, g_ref, wkt_ref, wvt_ref, mk_ref, mv_ref):
    mn = _rms(mem_ref[0], g_ref[...]).astype(BF16)
    mk_ref[0] = _dot_nt(wkt_ref[...], mn)
    mv_ref[0] = _dot_nt(wvt_ref[...], mn)


def _mem_kv(mem, p):
    b, n, d = mem.shape
    out = pl.BlockSpec((1, BRANCH_W, n), lambda i: (i, 0, 0))
    return _call(_mem_kv_kernel, grid=(b,),
                 in_specs=[pl.BlockSpec((1, n, d), lambda i: (i, 0, 0)), _resident((1, d)),
                           _resident((BRANCH_W, d)), _resident((BRANCH_W, d))],
                 out_specs=[out, out], out_shape=[jax.ShapeDtypeStruct((b, BRANCH_W, n), F32)] * 2,
                 name="mem_kv")(mem, p['g_mem'], p['wmkt'], p['wmvt'])


def _per_head_lanes(cols):
    tq = cols[0].shape[0]
    lane = lax.broadcasted_iota(jnp.int32, (tq, LANES), 1)
    lo = jnp.where(lane < HEAD_DIM, cols[0], cols[1])
    hi = jnp.where(lane < HEAD_DIM, cols[2], cols[3])
    return jnp.concatenate([lo, hi], axis=1)


def _mem_attn_kernel(q_ref, kt_ref, vt_ref, o_ref):
    tq = q_ref.shape[1]
    q = q_ref[0] * ATTN_SCALE
    qs = jnp.concatenate([jnp.where(_group_mask(q.shape, 1, HEAD_DIM, h), q, 0.0) for h in range(N_HEADS)], axis=0)
    s_all = _dot(qs.astype(BF16), kt_ref[0].astype(BF16))
    vt = vt_ref[0].astype(BF16)
    vbd = jnp.concatenate([jnp.where(_group_mask(vt.shape, 0, HEAD_DIM, h), vt, jnp.zeros_like(vt))
                           for h in range(N_HEADS)], axis=1)
    ps, inv = [], []
    for h in range(N_HEADS):
        s = s_all[h * tq:(h + 1) * tq]
        p = jnp.exp(s - jnp.max(s, axis=1, keepdims=True))
        inv.append(1.0 / jnp.sum(p, axis=1, keepdims=True))
        ps.append(p.astype(BF16))
    o_ref[0] = _dot_nt(jnp.concatenate(ps, axis=1), vbd) * _per_head_lanes(inv)


def _mem_prompt_kernel(qt_ref, kt_ref, vt_ref, o_ref, k_ref):
    tq = qt_ref.shape[2]

    @pl.when(pl.program_id(1) == 0)
    def _():
        k_ref[...] = kt_ref[0].T.astype(BF16)

    qt = qt_ref[0] * (ATTN_SCALE * LOG2E)
    qs = jnp.concatenate([jnp.where(_group_mask(qt.shape, 0, HEAD_DIM, h), qt, 0.0) for h in range(N_HEADS)], axis=1)
    s_all = _dot(k_ref[...], qs.astype(BF16))
    vt = vt_ref[0].astype(BF16)
    outs = []
    for h in range(N_HEADS):
        s = s_all[:, h * tq:(h + 1) * tq]
        p = jnp.exp2(s - jnp.max(s, axis=0, keepdims=True))
        inv = 1.0 / jnp.sum(p, axis=0, keepdims=True)
        outs.append(_dot(vt[_head_rows(h), :], p.astype(BF16)) * inv)
    o_ref[0] = jnp.concatenate(outs, axis=0).T


def _mem_attention_prompt(qt, kt, vt):
    b, _, s = qt.shape
    n = kt.shape[2]
    tq = 2 * ATT_BLOCK if s % (2 * ATT_BLOCK) == 0 else ATT_BLOCK
    kvspec = pl.BlockSpec((1, BRANCH_W, n), lambda bi, i: (bi, 0, 0))
    return _call(_mem_prompt_kernel, grid=(b, s // tq),
                 in_specs=[pl.BlockSpec((1, BRANCH_W, tq), lambda bi, i: (bi, 0, i)), kvspec, kvspec],
                 out_specs=pl.BlockSpec((1, tq, BRANCH_W), lambda bi, i: (bi, i, 0)),
                 out_shape=jax.ShapeDtypeStruct((b, s, BRANCH_W), F32),
                 scratch=[pltpu.VMEM((n, BRANCH_W), BF16)], name="mem_prompt")(qt, kt, vt)


def _mem_attention(q, kt, vt, tq, kv_base):
    b, s, _ = q.shape
    n = kt.shape[2]
    qspec = pl.BlockSpec((1, tq, BRANCH_W), lambda bi, i: (bi, i, 0))
    kvspec = pl.BlockSpec((1, BRANCH_W, n), lambda bi, i: (kv_base + bi, 0, 0))
    return _call(_mem_attn_kernel, grid=(b, s // tq), in_specs=[qspec, kvspec, kvspec], out_specs=qspec,
                 out_shape=jax.ShapeDtypeStruct(q.shape, F32), name="mem_attn")(q, kt, vt)


def _merge_kernel(lam_init, x_ref, fo_ref, do_ref, so_ref, mo_ref, gpre_ref, gdiff_ref, hsum_ref,
                  wb_ref, wg_ref, bg_ref, wo_ref, gpost_ref, out_ref):
    x = x_ref[...]
    d_model = x.shape[1]
    xn = _rms(x, gpre_ref[...]).astype(BF16)
    d = do_ref[...]
    hi, lo = _split_bf16(d * d)
    ms = (_dot(hi, hsum_ref[...]) + _dot(lo, hsum_ref[...])) * (1.0 / HEAD_DIM)
    dn = d * lax.rsqrt(ms + RMS_EPS) * gdiff_ref[...] * (1.0 - lam_init)
    acc = jnp.zeros((x.shape[0], d_model), F32)
    for n, o in enumerate((fo_ref[...], dn, so_ref[...], mo_ref[...])):
        proj = _dot(o.astype(BF16), wb_ref[n])
        gate = jax.nn.sigmoid(_dot(xn, wg_ref[:, n * d_model:(n + 1) * d_model]) + bg_ref[:, n * d_model:(n + 1) * d_model])
        acc = acc + gate * proj
    out_ref[...] = x + _rms(_dot(acc.astype(BF16), wo_ref[...]), gpost_ref[...])


def _merge(x2, fo, do, so, mo, p, lam_init):
    m, d = x2.shape
    tm = 256 if m % 256 == 0 else m
    row = lambda w: pl.BlockSpec((tm, w), lambda i: (i, 0))
    in_specs = [row(d)] + [row(BRANCH_W)] * 4 + [_resident((1, d)), _resident((1, BRANCH_W)), _resident((BRANCH_W, BRANCH_W)),
                                                  _resident(p['wb'].shape), _resident(p['wg'].shape), _resident(p['bg'].shape),
                                                  _resident(p['wo'].shape), _resident((1, d))]
    return _call(functools.partial(_merge_kernel, lam_init), grid=(m // tm,), in_specs=in_specs, out_specs=row(d),
                 out_shape=jax.ShapeDtypeStruct((m, d), F32), name="merge")(
        x2, fo, do, so, mo, p['g_mix_pre'], p['g_diff'], p['hsum'], p['wb'], p['wg'], p['bg'], p['wo'], p['g_mix_post'])


def _ffn_cols(c):
    return slice(c * FF_CHUNK, (c + 1) * FF_CHUNK)


def _ffn_up(hn, c, win_ref, d_ff):
    return (_dot(hn, win_ref[:, _ffn_cols(c)]),
            _dot(hn, win_ref[:, d_ff + c * FF_CHUNK:d_ff + (c + 1) * FF_CHUNK]))


def _ffn_down(a, a1, a2, u, c, cw_ref, cb_ref, wout_ref):
    cols = _ffn_cols(c)
    conv = cw_ref[0:1, cols] * a2 + cw_ref[1:2, cols] * a1 + cw_ref[2:3, cols] * a + cb_ref[:, cols]
    y = jax.nn.gelu(conv, approximate=True) * u
    return _dot(y.astype(BF16), wout_ref[cols, :])


def _ffn_prompt_kernel(per_seq, h_ref, gpre_ref, win_ref, cw_ref, cb_ref, wout_ref, gpost_ref, out_ref, conv_ref, carry_ref):
    h = h_ref[...]
    tm = h.shape[0]
    d_ff = cw_ref.shape[1]
    n_chunks = d_ff // FF_CHUNK
    hn = _rms(h, gpre_ref[...]).astype(BF16)

    @pl.when(pl.program_id(0) % per_seq == 0)
    def _():
        carry_ref[...] = jnp.zeros(carry_ref.shape, F32)

    row = lax.broadcasted_iota(jnp.int32, (tm, FF_CHUNK), 0)
    f = jnp.zeros(h.shape, F32)
    ahead = [_ffn_up(hn, c, win_ref, d_ff) for c in range(min(2, n_chunks))]
    for c in range(n_chunks):
        a, u = ahead.pop(0)
        if c + 2 < n_chunks:
            ahead.append(_ffn_up(hn, c + 2, win_ref, d_ff))
        cols = _ffn_cols(c)
        pm2, pm1 = carry_ref[0:1, cols], carry_ref[1:2, cols]
        a1 = jnp.where(row == 0, pm1, pltpu.roll(a, 1, 0))
        a2 = jnp.where(row == 0, pm2, jnp.where(row == 1, pm1, pltpu.roll(a, 2, 0)))
        f = f + _ffn_down(a, a1, a2, u, c, cw_ref, cb_ref, wout_ref)
        carry_ref[0:2, cols] = a[tm - 2:tm, :]
        conv_ref[0, :, cols] = a[tm - 2:tm, :]
    out_ref[...] = h + _rms(f, gpost_ref[...])


def _ffn_decode_kernel(h_ref, gpre_ref, win_ref, cw_ref, cb_ref, wout_ref, gpost_ref, p0_ref, p1_ref, out_ref, a_ref):
    h = h_ref[...]
    d_ff = cw_ref.shape[1]
    hn = _rms(h, gpre_ref[...]).astype(BF16)
    f = jnp.zeros(h.shape, F32)
    for c in range(d_ff // FF_CHUNK):
        cols = _ffn_cols(c)
        a, u = _ffn_up(hn, c, win_ref, d_ff)
        f = f + _ffn_down(a, p1_ref[:, cols], p0_ref[:, cols], u, c, cw_ref, cb_ref, wout_ref)
        a_ref[:, cols] = a
    out_ref[...] = h + _rms(f, gpost_ref[...])


def _ffn_weights_specs(p, d):
    return [_resident((1, d)), _resident(p['wfi'].shape), _resident(p['cw'].shape), _resident(p['cb'].shape),
            _resident(p['wfo'].shape), _resident((1, d))]


def _ffn_prompt(h2, batch, seq, p):
    m, d = h2.shape
    d_ff = p['cw'].shape[1]
    tm = 256 if seq % 256 == 0 else seq
    per_seq = seq // tm
    row = pl.BlockSpec((tm, d), lambda i: (i, 0))
    return _call(functools.partial(_ffn_prompt_kernel, per_seq), grid=(m // tm,),
                 in_specs=[row] + _ffn_weights_specs(p, d),
                 out_specs=[row, pl.BlockSpec((1, 2, d_ff), lambda i: (i // per_seq, 0, 0))],
                 out_shape=[jax.ShapeDtypeStruct((m, d), F32), jax.ShapeDtypeStruct((batch, 2, d_ff), F32)],
                 scratch=[pltpu.VMEM((SUBLANES, d_ff), F32)], name="ffn_prompt")(
        h2, p['g_ffn_pre'], p['wfi'], p['cw'], p['cb'], p['wfo'], p['g_ffn_post'])


def _ffn_decode(h2, prev, p):
    m, d = h2.shape
    d_ff = p['cw'].shape[1]
    full = lambda shape: pl.BlockSpec(shape, lambda i: (0,) * len(shape))
    y, a = _call(_ffn_decode_kernel, grid=(1,),
                 in_specs=[full((m, d))] + _ffn_weights_specs(p, d) + [full((m, d_ff)), full((m, d_ff))],
                 out_specs=[full((m, d)), full((m, d_ff))],
                 out_shape=[jax.ShapeDtypeStruct((m, d), F32), jax.ShapeDtypeStruct((m, d_ff), F32)],
                 name="ffn_decode")(h2, p['g_ffn_pre'], p['wfi'], p['cw'], p['cb'], p['wfo'], p['g_ffn_post'],
                                    prev[:, 0], prev[:, 1])
    return y, jnp.stack([prev[:, 1], a], axis=1)


def _row_query(q_ref, n_groups, scale):
    shape = (SUBLANES, BRANCH_W)
    width = BRANCH_W // n_groups
    own = lax.broadcasted_iota(jnp.int32, shape, 1) // width == lax.broadcasted_iota(jnp.int32, shape, 0)
    return jnp.where(own, jnp.broadcast_to(q_ref[...] * scale, shape), 0.0), own


def _gather_pages(kv_buf, slot, c, pg):
    return jnp.concatenate([kv_buf[slot, c, t] for t in range(pg)], axis=1).astype(BF16)


def _decode_softmax_step(s, vt, m_ref, l_ref, acc_ref):
    m_old = m_ref[:, 0:1]
    m_new = jnp.maximum(m_old, jnp.max(s, axis=1, keepdims=True))
    p = jnp.exp(s - m_new)
    alpha = jnp.exp(m_old - m_new)
    l_ref[:, 0:1] = alpha * l_ref[:, 0:1] + jnp.sum(p, axis=1, keepdims=True)
    m_ref[:, 0:1] = m_new
    acc_ref[...] = alpha * acc_ref[...] + _dot_nt(p.astype(BF16), vt)


def _decode_self_init(q8, kn_ref, vn_ref, m_ref, l_ref, acc_ref):
    m_ref[:, 0:1] = jnp.sum(q8 * kn_ref[...], axis=1, keepdims=True)
    l_ref[:, 0:1] = jnp.ones((SUBLANES, 1), F32)
    acc_ref[...] = jnp.broadcast_to(vn_ref[...], acc_ref.shape)


def _decode_kernel(pg, page_base, lam_init, pt_ref, fq_ref, dq_ref, sq_ref, fkn_ref, fvn_ref, dkn_ref, dvn_ref, lfn_ref,
                   lq1_ref, lk1_ref, lq2_ref, lk2_ref, lf_hbm, fk_hbm, fv_hbm, dk_hbm, dv_hbm, sk_hbm, sv_hbm,
                   fo_ref, do_ref, so_ref, fm_ref, fl_ref, facc_ref, fc_ref, lf_ref, dm_ref, dl_ref, dacc_ref,
                   sr_ref, sacc_ref, kv_buf, lf_buf, sem):
    steps = pl.num_programs(1)
    j = pl.program_id(1)
    first, last = j == 0, j == steps - 1
    n = pl.program_id(0) * steps + j
    slot = lax.rem(n, 2)

    def page_copies(step, into):
        b, group = lax.div(step, steps), steps - 1 - lax.rem(step, steps)
        copies = []
        for t in range(pg):
            page = page_base + pt_ref[b, group * pg + t]
            copies.append(pltpu.make_async_copy(lf_hbm.at[page], lf_buf.at[into, t], sem.at[into]))
            for c, hbm in enumerate((fk_hbm, fv_hbm, dk_hbm, dv_hbm, sk_hbm, sv_hbm)):
                copies.append(pltpu.make_async_copy(hbm.at[page], kv_buf.at[into, c, t], sem.at[into]))
        return copies

    @pl.when(n == 0)
    def _():
        for copy in page_copies(n, slot):
            copy.start()

    @pl.when(n + 1 < pl.num_programs(0) * steps)
    def _():
        for copy in page_copies(n + 1, 1 - slot):
            copy.start()

    for copy in page_copies(n, slot):
        copy.wait()

    fq8, f_own = _row_query(fq_ref, N_HEADS, ATTN_SCALE)
    dq8, _ = _row_query(dq_ref, 2 * N_HEADS, DIFF_SCALE)
    sq8, s_own = _row_query(sq_ref, N_HEADS, ATTN_SCALE)

    @pl.when(first)
    def _():
        _decode_self_init(fq8, fkn_ref, fvn_ref, fm_ref, fl_ref, facc_ref)
        _decode_self_init(dq8, dkn_ref, dvn_ref, dm_ref, dl_ref, dacc_ref)
        fc_ref[...] = jnp.zeros(fc_ref.shape, F32)
        fc_ref[0:N_HEADS, 0:1] = lfn_ref[...]
        lf_ref[...] = jnp.zeros(lf_ref.shape, F32)
        sr_ref[...] = jnp.zeros(sr_ref.shape, F32)
        sacc_ref[...] = jnp.zeros(sacc_ref.shape, F32)

    pages = functools.partial(_gather_pages, kv_buf, slot, pg=pg)
    for t in range(pg):
        lf_ref[0:N_HEADS, t * PAGE_SIZE:(t + 1) * PAGE_SIZE] = lf_buf[slot, t]
    lf = lf_ref[...]
    incl = _suffix_sum_lanes(lf)
    carry = fc_ref[:, 0:1]
    fc_ref[:, 0:1] = carry + incl[:, 0:1]
    s = _dot(fq8.astype(BF16), pages(c=0)) + (incl - lf + carry)
    _decode_softmax_step(s, pages(c=1), fm_ref, fl_ref, facc_ref)

    s = _dot(dq8.astype(BF16), pages(c=2))
    _decode_softmax_step(s, pages(c=3), dm_ref, dl_ref, dacc_ref)

    z = _dot(sq8.astype(BF16), pages(c=4))
    ls = _log_sigmoid(z)
    keep = ls - z
    incl = _suffix_sum_lanes(keep)
    r_old = sr_ref[:, 0:1]
    a = jnp.exp(ls + (incl - keep) + r_old)
    sr_ref[:, 0:1] = r_old + incl[:, 0:1]
    sacc_ref[...] += _dot_nt(a.astype(BF16), pages(c=5))

    @pl.when(last)
    def _():
        fo_ref[...] = jnp.sum(jnp.where(f_own, facc_ref[...] / fl_ref[:, 0:1], 0.0), axis=0, keepdims=True)
        lam = _diff_lambda(lq1_ref, lk1_ref, lq2_ref, lk2_ref, lam_init)
        shape = (SUBLANES, BRANCH_W)
        row = lax.broadcasted_iota(jnp.int32, shape, 0)
        head = lax.broadcasted_iota(jnp.int32, shape, 1) // HEAD_DIM
        o8 = dacc_ref[...] / dl_ref[:, 0:1] * jnp.where(row % 2 == 0, 1.0, -lam)
        do_ref[...] = jnp.sum(jnp.where(head == row // 2, o8, 0.0), axis=0, keepdims=True)
        so_ref[...] = jnp.sum(jnp.where(s_own, sacc_ref[...], 0.0), axis=0, keepdims=True)


def _decode_attention(page_table, page_base, queries, new_kv, lfn, lams, lf_cache, kv_caches, lam_init):
    db, n_pages = page_table.shape
    pg = PAGES_PER_STEP if n_pages % PAGES_PER_STEP == 0 else n_pages
    steps = n_pages // pg

    row = pl.BlockSpec((None, 1, BRANCH_W), lambda b, j, pt: (b, 0, 0))
    small = lambda shape: pl.BlockSpec(shape, lambda b, j, pt: (0,) * len(shape))
    in_hbm = pl.BlockSpec(memory_space=pl.ANY)
    in_specs = ([row] * 7 + [pl.BlockSpec((None, N_HEADS, 1), lambda b, j, pt: (b, 0, 0))] + [small((1, DIFF_DC))] * 4
                + [in_hbm] * 7)
    state = pltpu.VMEM((SUBLANES, LANES), F32)
    acc = pltpu.VMEM((SUBLANES, BRANCH_W), F32)
    scratch = [state, state, acc, state, pltpu.VMEM((SUBLANES, pg * PAGE_SIZE), F32), state, state, acc, state, acc,
               pltpu.VMEM((2, len(kv_caches), pg, BRANCH_W, PAGE_SIZE), F32),
               pltpu.VMEM((2, pg, N_HEADS, PAGE_SIZE), F32),
               pltpu.SemaphoreType.DMA((2,))]
    out = jax.ShapeDtypeStruct((db, 1, BRANCH_W), F32)
    return _call(functools.partial(_decode_kernel, pg, page_base, lam_init), grid=(db, steps), in_specs=in_specs,
                 out_specs=[row] * 3, out_shape=[out] * 3, scratch=scratch, prefetch=1,
                 name="decode_attn")(page_table, *queries, *new_kv, lfn, *lams, lf_cache, *kv_caches)


def _layer_params(l, g_mix_pre, g_mix_post, g_ffn_pre, g_ffn_post, g_mem, w_in, b_fox_f, g_diff, w_mem_k, w_mem_v,
                  w_branch, w_gate, b_gate, w_out, w_ffn_in, conv_w, conv_b, w_ffn_out):
    row = lambda a: a[l][None, :].astype(F32)
    wt = jnp.transpose(w_in[l]).astype(BF16)
    grp = lambda g: wt[g * BRANCH_W:(g + 1) * BRANCH_W]
    wft = jnp.zeros((SUBLANES, wt.shape[1]), BF16).at[:N_HEADS].set(wt[10 * BRANCH_W:])
    bf = jnp.zeros((SUBLANES, 1), F32).at[:N_HEADS, 0].set(b_fox_f[l])
    head = jnp.arange(BRANCH_W) // HEAD_DIM
    w_main = w_in[l][:, :10 * BRANCH_W].astype(BF16)
    cols = lambda g: w_main[:, g * BRANCH_W:(g + 1) * BRANCH_W]
    return dict(
        g_mix_pre=row(g_mix_pre), g_mix_post=row(g_mix_post), g_ffn_pre=row(g_ffn_pre), g_ffn_post=row(g_ffn_post),
        g_mem=row(g_mem), g_diff=jnp.tile(g_diff[l], N_HEADS)[None, :].astype(F32),
        wt=jnp.concatenate([grp(g) for g in (0, 3, 6, 9, 1, 2, 4, 5, 7, 8)], axis=0),
        w_all=jnp.concatenate([cols(g) for g in (0, 3, 6, 9, 1, 2, 4, 5, 7, 8)], axis=1),
        wft=wft, bf=bf,
        wmkt=jnp.transpose(w_mem_k[l]).astype(BF16), wmvt=jnp.transpose(w_mem_v[l]).astype(BF16),
        hsum=(head[:, None] == head[None, :]).astype(BF16),
        wb=w_branch[l].astype(BF16), wg=w_gate[l].astype(BF16), bg=row(b_gate), wo=w_out[l].astype(BF16),
        wfi=w_ffn_in[l].astype(BF16), cw=conv_w[l].astype(F32), cb=row(conv_b), wfo=w_ffn_out[l].astype(BF16))


def _feature_major_pages(cache):
    d, n, ps, h, e = cache.shape
    return jnp.transpose(cache, (0, 1, 3, 4, 2)).reshape(d * n, h * e, ps)


def _kv_out(a):
    d, b, _, s = a.shape
    return jnp.transpose(a.reshape(d, b, N_HEADS, HEAD_DIM, s), (0, 1, 4, 2, 3))


def kernel(x_prompt, mem_prompt, x_sample, cache_fox_k, cache_fox_v, cache_fox_logf, cache_diff_k, cache_diff_v,
           cache_sb_k, cache_sb_v, cache_mem_k, cache_mem_v, state_conv, page_table, g_mix_pre, g_mix_post,
           g_ffn_pre, g_ffn_post, g_mem, w_in, b_fox_f, diff_lq1, diff_lk1, diff_lq2, diff_lk2, g_diff, w_mem_k,
           w_mem_v, w_branch, w_gate, b_gate, w_out, w_ffn_in, conv_w, conv_b, w_ffn_out):
    depth = w_in.shape[0]
    batch, seq, d_model = x_prompt.shape
    db = x_sample.shape[0]
    n_pool = cache_fox_k.shape[1]
    past_len = page_table.shape[1] * PAGE_SIZE
    n_mem = cache_mem_k.shape[2]
    assert x_sample.shape[1] == 1 and seq % ATT_BLOCK == 0
    nq = seq // ATT_BLOCK

    kv_caches = [_feature_major_pages(c) for c in (cache_fox_k, cache_fox_v, cache_diff_k, cache_diff_v,
                                                   cache_sb_k, cache_sb_v)]
    lf_cache = jnp.transpose(cache_fox_logf, (0, 1, 3, 2)).reshape(depth * n_pool, N_HEADS, PAGE_SIZE)
    mem_kt = jnp.transpose(cache_mem_k, (0, 1, 3, 4, 2)).reshape(depth * db, BRANCH_W, n_mem)
    mem_vt = jnp.transpose(cache_mem_v, (0, 1, 3, 4, 2)).reshape(depth * db, BRANCH_W, n_mem)
    later = (jnp.arange(ATT_BLOCK)[:, None] < jnp.arange(ATT_BLOCK)[None, :]).astype(BF16)

    xp = x_prompt.reshape(batch * seq, d_model)
    xs = x_sample.reshape(db, d_model)
    p_rows, s_rows, kv_stacks = [], [], None
    for l in range(depth):
        lam_init = 0.8 - 0.6 * math.exp(-0.3 * l)
        p = _layer_params(l, g_mix_pre, g_mix_post, g_ffn_pre, g_ffn_post, g_mem, w_in, b_fox_f, g_diff, w_mem_k,
                          w_mem_v, w_branch, w_gate, b_gate, w_out, w_ffn_in, conv_w, conv_b, w_ffn_out)
        lams = [a[l][None, :].astype(F32) for a in (diff_lq1, diff_lk1, diff_lq2, diff_lk2)]

        fqt, dqt, sqt, mqt, *kv_stacks, lft = _project_prompt(xp, batch, seq, p, l, depth, kv_stacks)
        fkt, fvt, dkt, dvt, skt, svt = kv_stacks
        e = _fox_suffix(lft)
        ecol = jnp.transpose(e[:, :N_HEADS, :], (0, 2, 1))
        erow = jnp.transpose(e.reshape(batch, SUBLANES, nq, ATT_BLOCK), (0, 2, 1, 3))
        fox_o = _prompt_attention("fox", l, fqt, fkt, fvt, (ecol, erow))
        diff_o = _prompt_attention("diff", l, dqt, dkt, dvt, tuple(lams), lam_init)
        sb_o = _prompt_attention("sb", l, sqt, skt, svt, (later,))
        mkt, mvt = _mem_kv(mem_prompt, p)
        mem_o = _mem_attention_prompt(mqt, mkt, mvt)
        flat = lambda a: a.reshape(batch * seq, BRANCH_W)
        hp = _merge(xp, flat(fox_o), flat(diff_o), flat(sb_o), flat(mem_o), p, lam_init)
        xp, conv_p = _ffn_prompt(hp, batch, seq, p)
        p_rows.append((lft[:, :N_HEADS, :], mkt, mvt, conv_p))

        sfq, sdq, ssq, smq, sfk, sfv, sdk, sdv, ssk, ssv, slf = _project_decode(xs, past_len, p)
        one = lambda a: a.reshape(db, 1, BRANCH_W)
        slf4 = jnp.transpose(slf[:N_HEADS, :])
        fox_s, diff_s, sb_s = _decode_attention(
            page_table, l * n_pool, (one(sfq), one(sdq), one(ssq)), (one(sfk), one(sfv), one(sdk), one(sdv)),
            slf4[:, :, None], lams, lf_cache, kv_caches, lam_init)
        smq8 = jnp.zeros((db, SUBLANES, BRANCH_W), F32).at[:, 0, :].set(smq)
        mem_s = _mem_attention(smq8, mem_kt, mem_vt, SUBLANES, l * db)[:, 0, :]
        two = lambda a: a.reshape(db, BRANCH_W)
        hs = _merge(xs, two(fox_s), two(diff_s), two(sb_s), mem_s, p, lam_init)
        xs, conv_s = _ffn_decode(hs, state_conv[l], p)
        heads = lambda a: a.reshape(db, 1, N_HEADS, HEAD_DIM)
        s_rows.append((heads(sfk), heads(sfv), slf4[:, None, :], heads(sdk), heads(sdv), heads(ssk), heads(ssv), conv_s))

    p_fk, p_fv, p_dk, p_dv, p_sk, p_sv = (_kv_out(a) for a in kv_stacks)
    p_fl, p_mk, p_mv, p_cv = (jnp.stack(a) for a in zip(*p_rows))
    p_fl = jnp.transpose(p_fl, (0, 1, 3, 2))
    p_mk, p_mv = _kv_out(p_mk), _kv_out(p_mv)
    s_out = [jnp.stack(a) for a in zip(*s_rows)]
    return (xp.reshape(batch, seq, d_model), xs.reshape(db, 1, d_model), p_fk, p_fv, p_fl, p_dk, p_dv, p_sk, p_sv,
            p_mk, p_mv, p_cv, *s_out)
```

```python
import functools
import math

import jax
import jax.numpy as jnp
from jax import lax
from jax.experimental import pallas as pl
from jax.experimental.pallas import tpu as pltpu

F32 = jnp.float32
BF16 = jnp.bfloat16

HEAD_DIM = 64
N_HEADS = 4
BRANCH_W = N_HEADS * HEAD_DIM
DIFF_DC = HEAD_DIM // 2
N_BRANCH = 4
PAGE_SIZE = 128
ROPE_THETA = 500000.0
ROPE_ROT = DIFF_DC // 4
RMS_EPS = 1e-6
ATTN_SCALE = HEAD_DIM ** -0.5
DIFF_SCALE = DIFF_DC ** -0.5
NEG = -1e30
LOG2E = math.log2(math.e)

LANES = 128
SUBLANES = 8
VMEM_LIMIT_BYTES = 56 * 1024 * 1024
ATT_BLOCK = 256
FF_CHUNK = 256
PAGES_PER_STEP = 16


def _call(kernel, *, grid, in_specs, out_specs, out_shape, scratch=(), prefetch=0, name):
    spec = pltpu.PrefetchScalarGridSpec(num_scalar_prefetch=prefetch, grid=grid, in_specs=in_specs,
                                        out_specs=out_specs, scratch_shapes=list(scratch))
    params = pltpu.CompilerParams(dimension_semantics=("arbitrary",) * len(grid),
                                  vmem_limit_bytes=VMEM_LIMIT_BYTES)
    return pl.pallas_call(kernel, grid_spec=spec, out_shape=out_shape, compiler_params=params, name=name)


def _resident(shape):
    zeros = (0,) * len(shape)
    return pl.BlockSpec(shape, lambda *_: zeros, pipeline_mode=pl.Buffered(1))


def _rms(x, g):
    return x * lax.rsqrt(jnp.mean(x * x, axis=-1, keepdims=True) + RMS_EPS) * g


def _log_sigmoid(z):
    return jnp.minimum(z, 0.0) - jnp.log(1.0 + jnp.exp(-jnp.abs(z)))


def _dot(a, b):
    return jnp.dot(a, b, preferred_element_type=F32)


def _dot_nt(a, b):
    return lax.dot_general(a, b, (((1,), (1,)), ((), ())), preferred_element_type=F32)


def _split_bf16(x):
    hi = x.astype(BF16)
    lo = (x - hi.astype(F32)).astype(BF16)
    return hi, lo


def _group_mask(shape, axis, width, g):
    idx = lax.broadcasted_iota(jnp.int32, shape, axis)
    return (idx >= g * width) & (idx < (g + 1) * width)


def _set_row(state, h, row):
    return jnp.where(lax.broadcasted_iota(jnp.int32, state.shape, 0) == h, row, state)


def _rope_rows(h, c, s1, s2):
    return h * c + pltpu.roll(h, BRANCH_W - ROPE_ROT // 2, 1) * s1 + pltpu.roll(h, ROPE_ROT // 2, 1) * s2


def _rope_cols(h, c, s1, s2):
    return h * c + pltpu.roll(h, BRANCH_W - ROPE_ROT // 2, 0) * s1 + pltpu.roll(h, ROPE_ROT // 2, 0) * s2


def _proj_prompt_kernel(n_carried, x_ref, g_ref, wt_ref, wft_ref, bf_ref, rct_ref, rs1t_ref, rs2t_ref, *refs):
    fq_ref, dq_ref, sq_ref, mq_ref, fk_ref, fv_ref, dk_ref, dv_ref, sk_ref, sv_ref, lf_ref = refs[n_carried:]
    xn = _rms(x_ref[...], g_ref[...]).astype(BF16)
    outs = (fq_ref, dq_ref, sq_ref, mq_ref, fk_ref, fv_ref, dk_ref, dv_ref, sk_ref, sv_ref)
    for gi, out in enumerate(outs):
        ht = _dot_nt(wt_ref[gi * BRANCH_W:(gi + 1) * BRANCH_W, :], xn)
        if out is dq_ref or out is dk_ref:
            ht = _rope_cols(ht, rct_ref[...], rs1t_ref[...], rs2t_ref[...])
        out[0] = ht
        for later_layer in range(1, out.shape[0]):
            out[later_layer] = jnp.zeros_like(ht)
    lf_ref[0] = _log_sigmoid(_dot_nt(wft_ref[...], xn) + bf_ref[...])


def _proj_decode_kernel(x_ref, g_ref, w_ref, wft_ref, bf_ref, rc_ref, rs1_ref, rs2_ref, *outs):
    xn = _rms(x_ref[...], g_ref[...]).astype(BF16)
    for gi in range(10):
        h = _dot(xn, w_ref[:, gi * BRANCH_W:(gi + 1) * BRANCH_W])
        if gi in (1, 6):
            h = _rope_rows(h, rc_ref[...], rs1_ref[...], rs2_ref[...])
        outs[gi][...] = h
    outs[10][...] = _log_sigmoid(_dot_nt(wft_ref[...], xn) + bf_ref[...])


def _rope_tables(pos):
    half = ROPE_ROT // 2
    inv_freq = ROPE_THETA ** (-jnp.arange(half, dtype=F32) * 2.0 / ROPE_ROT)
    ang = pos.astype(F32)[:, None] * inv_freq[None, :]
    cos, sin = jnp.cos(ang), jnp.sin(ang)
    n = pos.shape[0]
    pad = jnp.zeros((n, DIFF_DC - ROPE_ROT), F32)
    c = jnp.concatenate([cos, cos, pad + 1.0], axis=1)
    s1 = jnp.concatenate([-sin, jnp.zeros_like(sin), pad], axis=1)
    s2 = jnp.concatenate([jnp.zeros_like(sin), sin, pad], axis=1)
    reps = BRANCH_W // DIFF_DC
    return tuple(jnp.tile(t, (1, reps)) for t in (c, s1, s2))


def _project_prompt(x2, batch, seq, p, layer, depth, kv_stacks):
    t = batch * seq
    tm = 512 if seq % 512 == 0 else seq
    per_seq = seq // tm
    rct, rs1t, rs2t = (a.T for a in _rope_tables(jnp.arange(seq)))
    d = x2.shape[1]
    row = lambda i: (i, 0)
    tabt = lambda i: (0, i % per_seq)
    fm = lambda i: (i // per_seq, 0, i % per_seq)
    carried = list(kv_stacks or ())
    if carried:
        kv_spec = pl.BlockSpec((None, 1, BRANCH_W, tm), lambda i: (layer, i // per_seq, 0, i % per_seq))
    else:
        assert layer == 0
        kv_spec = pl.BlockSpec((depth, None, BRANCH_W, tm), lambda i: (0, i // per_seq, 0, i % per_seq))
    in_specs = [pl.BlockSpec((tm, d), row), _resident((1, d)), _resident(p['wt'].shape),
                _resident(p['wft'].shape), _resident((SUBLANES, 1)),
                pl.BlockSpec((BRANCH_W, tm), tabt), pl.BlockSpec((BRANCH_W, tm), tabt), pl.BlockSpec((BRANCH_W, tm), tabt)]
    in_specs += [pl.BlockSpec(memory_space=pl.ANY)] * len(carried)
    out_specs = ([pl.BlockSpec((1, BRANCH_W, tm), fm)] * 4 + [kv_spec] * 6 + [pl.BlockSpec((1, SUBLANES, tm), fm)])
    out_shape = ([jax.ShapeDtypeStruct((batch, BRANCH_W, seq), F32)] * 4
                 + [jax.ShapeDtypeStruct((depth, batch, BRANCH_W, seq), F32)] * 6
                 + [jax.ShapeDtypeStruct((batch, SUBLANES, seq), F32)])
    n_fixed = len(in_specs) - len(carried)
    spec = pltpu.PrefetchScalarGridSpec(num_scalar_prefetch=0, grid=(t // tm,), in_specs=in_specs, out_specs=out_specs)
    call = pl.pallas_call(
        functools.partial(_proj_prompt_kernel, len(carried)), grid_spec=spec, out_shape=out_shape,
        input_output_aliases={n_fixed + n: 4 + n for n in range(len(carried))},
        compiler_params=pltpu.CompilerParams(dimension_semantics=("arbitrary",), vmem_limit_bytes=VMEM_LIMIT_BYTES),
        name="proj_prompt")
    return call(x2, p['g_mix_pre'], p['wt'], p['wft'], p['bf'], rct, rs1t, rs2t, *carried)


def _project_decode(x2, pos, p):
    m, d = x2.shape
    rc, rs1, rs2 = _rope_tables(jnp.full((m,), pos))
    full = lambda shape: pl.BlockSpec(shape, lambda i: (0,) * len(shape))
    in_specs = [full((m, d)), full((1, d)), full(p['w_all'].shape), full(p['wft'].shape), full((SUBLANES, 1)),
                full((m, BRANCH_W)), full((m, BRANCH_W)), full((m, BRANCH_W))]
    out_specs = [full((m, BRANCH_W))] * 10 + [full((SUBLANES, m))]
    out_shape = [jax.ShapeDtypeStruct((m, BRANCH_W), F32)] * 10 + [jax.ShapeDtypeStruct((SUBLANES, m), F32)]
    return _call(_proj_decode_kernel, grid=(1,), in_specs=in_specs, out_specs=out_specs, out_shape=out_shape,
                 name="proj_decode")(x2, p['g_mix_pre'], p['w_all'], p['wft'], p['bf'], rc, rs1, rs2)


def _suffix_sum_lanes(x):
    n = x.shape[1]
    lane = lax.broadcasted_iota(jnp.int32, x.shape, 1)
    d = 1
    while d < n:
        x = x + jnp.where(lane + d < n, pltpu.roll(x, n - d, 1), 0.0)
        d *= 2
    return x


def _suffix_kernel(lf_ref, e_ref):
    lf = lf_ref[0]
    e_ref[0] = _suffix_sum_lanes(lf) - lf


def _fox_suffix(lft):
    b, r, s = lft.shape
    spec = pl.BlockSpec((1, r, s), lambda i: (i, 0, 0))
    return _call(_suffix_kernel, grid=(b,), in_specs=[spec], out_specs=spec,
                 out_shape=jax.ShapeDtypeStruct(lft.shape, F32), name="fox_suffix")(lft)


def _build_q_stack(qt_ref, qs_ref, n_groups, scale, tq):
    nq = qs_ref.shape[0]
    width = BRANCH_W // n_groups
    for i in range(nq):
        qt = qt_ref[0, :, i * tq:(i + 1) * tq] * scale
        for g in range(n_groups):
            qs_ref[i, :, g * tq:(g + 1) * tq] = jnp.where(_group_mask(qt.shape, 0, width, g), qt, 0.0).astype(BF16)


def _visit_query_blocks(stages, first, nq):
    def run(blocks, masks):
        carries = [None] * len(blocks)
        for stage in stages:
            carries = [stage(i, c, m) for i, c, m in zip(blocks, carries, masks)]

    rest = nq - 1 - first
    odd = lax.rem(rest, 2)

    @pl.when(odd == 1)
    def _():
        run([first, first + 1], [True, False])

    @pl.when(odd == 0)
    def _():
        run([first], [True])

    start = first + 1 + odd

    def pair(t, c):
        run([start + 2 * t, start + 2 * t + 1], [False, False])
        return c

    lax.fori_loop(0, lax.div(rest, 2), pair, 0)


def _softmax_step(u, m_old, l_old, shift=None):
    mu = jnp.max(u, axis=0, keepdims=True)
    if shift is not None:
        mu = mu - shift
    m_new = jnp.maximum(m_old, mu)
    p = jnp.exp2(u - (m_new if shift is None else m_new + shift))
    alpha = jnp.exp2(m_old - m_new)
    return p.astype(BF16), alpha, m_new, alpha * l_old + jnp.sum(p, axis=0, keepdims=True)


def _head_rows(h):
    return slice(h * HEAD_DIM, (h + 1) * HEAD_DIM)


def _fox_prompt_kernel(qt_ref, kt_ref, vt_ref, ecol_ref, erow_ref, o_ref, qs_ref, acc_ref, m_ref, l_ref):
    tq = tk = ATT_BLOCK
    nq = qs_ref.shape[0]
    j = pl.program_id(1)

    @pl.when(j == 0)
    def _():
        _build_q_stack(qt_ref, qs_ref, N_HEADS, ATTN_SCALE * LOG2E, tq)
        m_ref[...] = jnp.full(m_ref.shape, NEG, F32)
        l_ref[...] = jnp.zeros(l_ref.shape, F32)
        acc_ref[...] = jnp.zeros(acc_ref.shape, F32)

    k = kt_ref[0].T.astype(BF16)
    vt = vt_ref[0].astype(BF16)
    ek = [jnp.broadcast_to(ecol_ref[0, :, h:h + 1] * LOG2E, (tk, tq)) for h in range(N_HEADS)]
    causal = lax.broadcasted_iota(jnp.int32, (tk, tq), 0) <= lax.broadcasted_iota(jnp.int32, (tk, tq), 1)

    def scores(i, _, masked):
        return _dot(k, qs_ref[i])

    def update(i, s_all, masked):
        m_all, l_all, eq_all = m_ref[i], l_ref[i], erow_ref[0, i] * LOG2E
        for h in range(N_HEADS):
            u = s_all[:, h * tq:(h + 1) * tq] + ek[h]
            if masked:
                u = jnp.where(causal, u, NEG)
            p, alpha, m_new, l_new = _softmax_step(u, m_all[h:h + 1], l_all[h:h + 1], shift=eq_all[h:h + 1])
            acc_ref[i, _head_rows(h), :] = acc_ref[i, _head_rows(h), :] * alpha + _dot(vt[_head_rows(h), :], p)
            m_all, l_all = _set_row(m_all, h, m_new), _set_row(l_all, h, l_new)
        m_ref[i], l_ref[i] = m_all, l_all

    _visit_query_blocks((scores, update), j, nq)

    @pl.when(j == pl.num_programs(1) - 1)
    def _():
        for i in range(nq):
            l_all = l_ref[i]
            ot = jnp.concatenate([acc_ref[i, _head_rows(h), :] * (1.0 / l_all[h:h + 1]) for h in range(N_HEADS)], axis=0)
            o_ref[0, i * tq:(i + 1) * tq, :] = ot.T


def _diff_lambda(lq1_ref, lk1_ref, lq2_ref, lk2_ref, lam_init):
    e1 = jnp.exp(jnp.sum(lq1_ref[...] * lk1_ref[...], axis=1, keepdims=True))
    e2 = jnp.exp(jnp.sum(lq2_ref[...] * lk2_ref[...], axis=1, keepdims=True))
    return e1 - e2 + lam_init


def _diff_prompt_kernel(lam_init, qt_ref, kt_ref, vt_ref, lq1_ref, lk1_ref, lq2_ref, lk2_ref, o_ref,
                        qs_ref, acc1_ref, acc2_ref, m_ref, l_ref):
    tq = tk = ATT_BLOCK
    nq = qs_ref.shape[0]
    j = pl.program_id(1)

    @pl.when(j == 0)
    def _():
        _build_q_stack(qt_ref, qs_ref, 2 * N_HEADS, DIFF_SCALE * LOG2E, tq)
        m_ref[...] = jnp.full(m_ref.shape, NEG, F32)
        l_ref[...] = jnp.zeros(l_ref.shape, F32)
        acc1_ref[...] = jnp.zeros(acc1_ref.shape, F32)
        acc2_ref[...] = jnp.zeros(acc2_ref.shape, F32)

    k = kt_ref[0].T.astype(BF16)
    vt = vt_ref[0].astype(BF16)
    causal = lax.broadcasted_iota(jnp.int32, (tk, tq), 0) <= lax.broadcasted_iota(jnp.int32, (tk, tq), 1)

    def scores(i, _, masked):
        return _dot(k, qs_ref[i])

    def update(i, s_all, masked):
        m_all, l_all = m_ref[i], l_ref[i]
        for h in range(N_HEADS):
            ps, alphas = [], []
            for g in (2 * h, 2 * h + 1):
                u = s_all[:, g * tq:(g + 1) * tq]
                if masked:
                    u = jnp.where(causal, u, NEG)
                p, alpha, m_new, l_new = _softmax_step(u, m_all[g:g + 1], l_all[g:g + 1])
                m_all, l_all = _set_row(m_all, g, m_new), _set_row(l_all, g, l_new)
                ps.append(p)
                alphas.append(alpha)
            pv = _dot(vt[_head_rows(h), :], jnp.concatenate(ps, axis=1))
            acc1_ref[i, _head_rows(h), :] = acc1_ref[i, _head_rows(h), :] * alphas[0] + pv[:, :tq]
            acc2_ref[i, _head_rows(h), :] = acc2_ref[i, _head_rows(h), :] * alphas[1] + pv[:, tq:]
        m_ref[i], l_ref[i] = m_all, l_all

    _visit_query_blocks((scores, update), j, nq)

    @pl.when(j == pl.num_programs(1) - 1)
    def _():
        lam = _diff_lambda(lq1_ref, lk1_ref, lq2_ref, lk2_ref, lam_init)
        for i in range(nq):
            l_all = l_ref[i]
            ot = jnp.concatenate(
                [acc1_ref[i, _head_rows(h), :] * (1.0 / l_all[2 * h:2 * h + 1])
                 - lam * (acc2_ref[i, _head_rows(h), :] * (1.0 / l_all[2 * h + 1:2 * h + 2])) for h in range(N_HEADS)], axis=0)
            o_ref[0, i * tq:(i + 1) * tq, :] = ot.T


def _sb_prompt_kernel(qt_ref, kt_ref, vt_ref, later_ref, o_ref, qs_ref, acc_ref, r_ref):
    tq = tk = ATT_BLOCK
    nq = qs_ref.shape[0]
    j = pl.program_id(1)
    jb = pl.num_programs(1) - 1 - j

    @pl.when(j == 0)
    def _():
        _build_q_stack(qt_ref, qs_ref, N_HEADS, ATTN_SCALE * LOG2E, tq)
        r_ref[...] = jnp.zeros(r_ref.shape, F32)
        acc_ref[...] = jnp.zeros(acc_ref.shape, F32)

    k = kt_ref[0].T.astype(BF16)
    vt = vt_ref[0].astype(BF16)
    later = later_ref[...]
    strict = lax.broadcasted_iota(jnp.int32, (tk, tq), 0) < lax.broadcasted_iota(jnp.int32, (tk, tq), 1)

    def scores(i, _, masked):
        return _dot(k, qs_ref[i])

    def gates(i, z_all, masked):
        r_all = r_ref[i]
        log_beta, keeps = [], []
        for h in range(N_HEADS):
            z = z_all[:, h * tq:(h + 1) * tq]
            ls = jnp.minimum(z, 0.0) - jnp.log2(1.0 + jnp.exp2(-jnp.abs(z)))
            keep = ls - z
            if masked:
                keep = jnp.where(strict, keep, 0.0)
            keeps.append(keep.astype(BF16))
            log_beta.append(ls + r_all[h:h + 1])
            r_all = _set_row(r_all, h, r_all[h:h + 1] + jnp.sum(keep, axis=0, keepdims=True))
        r_ref[i] = r_all
        tails = _dot(later, jnp.concatenate(keeps, axis=1))
        return log_beta, tails

    def accumulate(i, carry, masked):
        log_beta, tails = carry
        for h in range(N_HEADS):
            a = jnp.exp2(log_beta[h] + tails[:, h * tq:(h + 1) * tq])
            if masked:
                a = jnp.where(strict, a, 0.0)
            acc_ref[i, _head_rows(h), :] += _dot(vt[_head_rows(h), :], a.astype(BF16))

    _visit_query_blocks((scores, gates, accumulate), jb, nq)

    @pl.when(j == pl.num_programs(1) - 1)
    def _():
        for i in range(nq):
            o_ref[0, i * tq:(i + 1) * tq, :] = acc_ref[i].T


def _prompt_attention(kind, layer, qt, kt, vt, extra, lam_init=None):
    b, _, s = qt.shape
    tq = tk = ATT_BLOCK
    nq, nk = s // tq, s // tk
    kblk = (lambda bi, j: nk - 1 - j) if kind == "sb" else (lambda bi, j: j)
    whole = pl.BlockSpec((1, BRANCH_W, s), lambda bi, j: (bi, 0, 0))
    kv_spec = pl.BlockSpec((None, 1, BRANCH_W, tk), lambda bi, j: (layer, bi, 0, kblk(bi, j)))
    in_specs = [whole, kv_spec, kv_spec]
    state = pltpu.VMEM((nq, SUBLANES, tq), F32)
    acc = pltpu.VMEM((nq, BRANCH_W, tq), F32)
    if kind == "fox":
        in_specs += [pl.BlockSpec((1, tk, N_HEADS), lambda bi, j: (bi, j, 0)),
                     pl.BlockSpec((1, nq, SUBLANES, tq), lambda bi, j: (bi, 0, 0, 0))]
        kernel, groups, scratch = _fox_prompt_kernel, N_HEADS, [acc, state, state]
    elif kind == "diff":
        in_specs += [_resident((1, DIFF_DC))] * 4
        kernel, groups, scratch = functools.partial(_diff_prompt_kernel, lam_init), 2 * N_HEADS, [acc, acc, state, state]
    else:
        in_specs += [_resident((tk, tk))]
        kernel, groups, scratch = _sb_prompt_kernel, N_HEADS, [acc, state]
    scratch = [pltpu.VMEM((nq, BRANCH_W, groups * tq), BF16)] + scratch
    return _call(kernel, grid=(b, nk), in_specs=in_specs,
                 out_specs=pl.BlockSpec((1, s, BRANCH_W), lambda bi, j: (bi, 0, 0)),
                 out_shape=jax.ShapeDtypeStruct((b, s, BRANCH_W), F32), scratch=scratch,
                 name=kind + "_prompt")(qt, kt, vt, *extra)


def _mem_kv_kernel(mem_ref, g_ref, wkt_ref, wvt_ref, mk_ref, mv_ref):
    mn = _rms(mem_ref[0], g_ref[...]).astype(BF16)
    mk_ref[0] = _dot_nt(wkt_ref[...], mn)
    mv_ref[0] = _dot_nt(wvt_ref[...], mn)


def _mem_kv(mem, p):
    b, n, d = mem.shape
    out = pl.BlockSpec((1, BRANCH_W, n), lambda i: (i, 0, 0))
    return _call(_mem_kv_kernel, grid=(b,),
                 in_specs=[pl.BlockSpec((1, n, d), lambda i: (i, 0, 0)), _resident((1, d)),
                           _resident((BRANCH_W, d)), _resident((BRANCH_W, d))],
                 out_specs=[out, out], out_shape=[jax.ShapeDtypeStruct((b, BRANCH_W, n), F32)] * 2,
                 name="mem_kv")(mem, p['g_mem'], p['wmkt'], p['wmvt'])


def _per_head_lanes(cols):
    tq = cols[0].shape[0]
    lane = lax.broadcasted_iota(jnp.int32, (tq, LANES), 1)
    lo = jnp.where(lane < HEAD_DIM, cols[0], cols[1])
    hi = jnp.where(lane < HEAD_DIM, cols[2], cols[3])
    return jnp.concatenate([lo, hi], axis=1)


def _mem_attn_kernel(q_ref, kt_ref, vt_ref, o_ref):
    tq = q_ref.shape[1]
    q = q_ref[0] * ATTN_SCALE
    qs = jnp.concatenate([jnp.where(_group_mask(q.shape, 1, HEAD_DIM, h), q, 0.0) for h in range(N_HEADS)], axis=0)
    s_all = _dot(qs.astype(BF16), kt_ref[0].astype(BF16))
    vt = vt_ref[0].astype(BF16)
    vbd = jnp.concatenate([jnp.where(_group_mask(vt.shape, 0, HEAD_DIM, h), vt, jnp.zeros_like(vt))
                           for h in range(N_HEADS)], axis=1)
    ps, inv = [], []
    for h in range(N_HEADS):
        s = s_all[h * tq:(h + 1) * tq]
        p = jnp.exp(s - jnp.max(s, axis=1, keepdims=True))
        inv.append(1.0 / jnp.sum(p, axis=1, keepdims=True))
        ps.append(p.astype(BF16))
    o_ref[0] = _dot_nt(jnp.concatenate(ps, axis=1), vbd) * _per_head_lanes(inv)


def _mem_prompt_kernel(qt_ref, kt_ref, vt_ref, o_ref, k_ref):
    tq = qt_ref.shape[2]

    @pl.when(pl.program_id(1) == 0)
    def _():
        k_ref[...] = kt_ref[0].T.astype(BF16)

    qt = qt_ref[0] * (ATTN_SCALE * LOG2E)
    qs = jnp.concatenate([jnp.where(_group_mask(qt.shape, 0, HEAD_DIM, h), qt, 0.0) for h in range(N_HEADS)], axis=1)
    s_all = _dot(k_ref[...], qs.astype(BF16))
    vt = vt_ref[0].astype(BF16)
    outs = []
    for h in range(N_HEADS):
        s = s_all[:, h * tq:(h + 1) * tq]
        p = jnp.exp2(s - jnp.max(s, axis=0, keepdims=True))
        inv = 1.0 / jnp.sum(p, axis=0, keepdims=True)
        outs.append(_dot(vt[_head_rows(h), :], p.astype(BF16)) * inv)
    o_ref[0] = jnp.concatenate(outs, axis=0).T


def _mem_attention_prompt(qt, kt, vt):
    b, _, s = qt.shape
    n = kt.shape[2]
    tq = 2 * ATT_BLOCK if s % (2 * ATT_BLOCK) == 0 else ATT_BLOCK
    kvspec = pl.BlockSpec((1, BRANCH_W, n), lambda bi, i: (bi, 0, 0))
    return _call(_mem_prompt_kernel, grid=(b, s // tq),
                 in_specs=[pl.BlockSpec((1, BRANCH_W, tq), lambda bi, i: (bi, 0, i)), kvspec, kvspec],
                 out_specs=pl.BlockSpec((1, tq, BRANCH_W), lambda bi, i: (bi, i, 0)),
                 out_shape=jax.ShapeDtypeStruct((b, s, BRANCH_W), F32),
                 scratch=[pltpu.VMEM((n, BRANCH_W), BF16)], name="mem_prompt")(qt, kt, vt)


def _mem_attention(q, kt, vt, tq, kv_base):
    b, s, _ = q.shape
    n = kt.shape[2]
    qspec = pl.BlockSpec((1, tq, BRANCH_W), lambda bi, i: (bi, i, 0))
    kvspec = pl.BlockSpec((1, BRANCH_W, n), lambda bi, i: (kv_base + bi, 0, 0))
    return _call(_mem_attn_kernel, grid=(b, s // tq), in_specs=[qspec, kvspec, kvspec], out_specs=qspec,
                 out_shape=jax.ShapeDtypeStruct(q.shape, F32), name="mem_attn")(q, kt, vt)


def _merge_kernel(lam_init, x_ref, fo_ref, do_ref, so_ref, mo_ref, gpre_ref, gdiff_ref, hsum_ref,
                  wb_ref, wg_ref, bg_ref, wo_ref, gpost_ref, out_ref):
    x = x_ref[...]
    d_model = x.shape[1]
    xn = _rms(x, gpre_ref[...]).astype(BF16)
    d = do_ref[...]
    hi, lo = _split_bf16(d * d)
    ms = (_dot(hi, hsum_ref[...]) + _dot(lo, hsum_ref[...])) * (1.0 / HEAD_DIM)
    dn = d * lax.rsqrt(ms + RMS_EPS) * gdiff_ref[...] * (1.0 - lam_init)
    acc = jnp.zeros((x.shape[0], d_model), F32)
    for n, o in enumerate((fo_ref[...], dn, so_ref[...], mo_ref[...])):
        proj = _dot(o.astype(BF16), wb_ref[n])
        gate = jax.nn.sigmoid(_dot(xn, wg_ref[:, n * d_model:(n + 1) * d_model]) + bg_ref[:, n * d_model:(n + 1) * d_model])
        acc = acc + gate * proj
    out_ref[...] = x + _rms(_dot(acc.astype(BF16), wo_ref[...]), gpost_ref[...])


def _merge(x2, fo, do, so, mo, p, lam_init):
    m, d = x2.shape
    tm = 256 if m % 256 == 0 else m
    row = lambda w: pl.BlockSpec((tm, w), lambda i: (i, 0))
    in_specs = [row(d)] + [row(BRANCH_W)] * 4 + [_resident((1, d)), _resident((1, BRANCH_W)), _resident((BRANCH_W, BRANCH_W)),
                                                  _resident(p['wb'].shape), _resident(p['wg'].shape), _resident(p['bg'].shape),
                                                  _resident(p['wo'].shape), _resident((1, d))]
    return _call(functools.partial(_merge_kernel, lam_init), grid=(m // tm,), in_specs=in_specs, out_specs=row(d),
                 out_shape=jax.ShapeDtypeStruct((m, d), F32), name="merge")(
        x2, fo, do, so, mo, p['g_mix_pre'], p['g_diff'], p['hsum'], p['wb'], p['wg'], p['bg'], p['wo'], p['g_mix_post'])


def _ffn_cols(c):
    return slice(c * FF_CHUNK, (c + 1) * FF_CHUNK)


def _ffn_up(hn, c, win_ref, d_ff):
    return (_dot(hn, win_ref[:, _ffn_cols(c)]),
            _dot(hn, win_ref[:, d_ff + c * FF_CHUNK:d_ff + (c + 1) * FF_CHUNK]))


def _ffn_down(a, a1, a2, u, c, cw_ref, cb_ref, wout_ref):
    cols = _ffn_cols(c)
    conv = cw_ref[0:1, cols] * a2 + cw_ref[1:2, cols] * a1 + cw_ref[2:3, cols] * a + cb_ref[:, cols]
    y = jax.nn.gelu(conv, approximate=True) * u
    return _dot(y.astype(BF16), wout_ref[cols, :])


def _ffn_prompt_kernel(per_seq, h_ref, gpre_ref, win_ref, cw_ref, cb_ref, wout_ref, gpost_ref, out_ref, conv_ref, carry_ref):
    h = h_ref[...]
    tm = h.shape[0]
    d_ff = cw_ref.shape[1]
    n_chunks = d_ff // FF_CHUNK
    hn = _rms(h, gpre_ref[...]).astype(BF16)

    @pl.when(pl.program_id(0) % per_seq == 0)
    def _():
        carry_ref[...] = jnp.zeros(carry_ref.shape, F32)

    row = lax.broadcasted_iota(jnp.int32, (tm, FF_CHUNK), 0)
    f = jnp.zeros(h.shape, F32)
    ahead = [_ffn_up(hn, c, win_ref, d_ff) for c in range(min(2, n_chunks))]
    for c in range(n_chunks):
        a, u = ahead.pop(0)
        if c + 2 < n_chunks:
            ahead.append(_ffn_up(hn, c + 2, win_ref, d_ff))
        cols = _ffn_cols(c)
        pm2, pm1 = carry_ref[0:1, cols], carry_ref[1:2, cols]
        a1 = jnp.where(row == 0, pm1, pltpu.roll(a, 1, 0))
        a2 = jnp.where(row == 0, pm2, jnp.where(row == 1, pm1, pltpu.roll(a, 2, 0)))
        f = f + _ffn_down(a, a1, a2, u, c, cw_ref, cb_ref, wout_ref)
        carry_ref[0:2, cols] = a[tm - 2:tm, :]
        conv_ref[0, :, cols] = a[tm - 2:tm, :]
    out_ref[...] = h + _rms(f, gpost_ref[...])


def _ffn_decode_kernel(h_ref, gpre_ref, win_ref, cw_ref, cb_ref, wout_ref, gpost_ref, p0_ref, p1_ref, out_ref, a_ref):
    h = h_ref[...]
    d_ff = cw_ref.shape[1]
    hn = _rms(h, gpre_ref[...]).astype(BF16)
    f = jnp.zeros(h.shape, F32)
    for c in range(d_ff // FF_CHUNK):
        cols = _ffn_cols(c)
        a, u = _ffn_up(hn, c, win_ref, d_ff)
        f = f + _ffn_down(a, p1_ref[:, cols], p0_ref[:, cols], u, c, cw_ref, cb_ref, wout_ref)
        a_ref[:, cols] = a
    out_ref[...] = h + _rms(f, gpost_ref[...])


def _ffn_weights_specs(p, d):
    return [_resident((1, d)), _resident(p['wfi'].shape), _resident(p['cw'].shape), _resident(p['cb'].shape),
            _resident(p['wfo'].shape), _resident((1, d))]


def _ffn_prompt(h2, batch, seq, p):
    m, d = h2.shape
    d_ff = p['cw'].shape[1]
    tm = 256 if seq % 256 == 0 else seq
    per_seq = seq // tm
    row = pl.BlockSpec((tm, d), lambda i: (i, 0))
    return _call(functools.partial(_ffn_prompt_kernel, per_seq), grid=(m // tm,),
                 in_specs=[row] + _ffn_weights_specs(p, d),
                 out_specs=[row, pl.BlockSpec((1, 2, d_ff), lambda i: (i // per_seq, 0, 0))],
                 out_shape=[jax.ShapeDtypeStruct((m, d), F32), jax.ShapeDtypeStruct((batch, 2, d_ff), F32)],
                 scratch=[pltpu.VMEM((SUBLANES, d_ff), F32)], name="ffn_prompt")(
        h2, p['g_ffn_pre'], p['wfi'], p['cw'], p['cb'], p['wfo'], p['g_ffn_post'])


def _ffn_decode(h2, prev, p):
    m, d = h2.shape
    d_ff = p['cw'].shape[1]
    full = lambda shape: pl.BlockSpec(shape, lambda i: (0,) * len(shape))
    y, a = _call(_ffn_decode_kernel, grid=(1,),
                 in_specs=[full((m, d))] + _ffn_weights_specs(p, d) + [full((m, d_ff)), full((m, d_ff))],
                 out_specs=[full((m, d)), full((m, d_ff))],
                 out_shape=[jax.ShapeDtypeStruct((m, d), F32), jax.ShapeDtypeStruct((m, d_ff), F32)],
                 name="ffn_decode")(h2, p['g_ffn_pre'], p['wfi'], p['cw'], p['cb'], p['wfo'], p['g_ffn_post'],
                                    prev[:, 0], prev[:, 1])
    return y, jnp.stack([prev[:, 1], a], axis=1)


def _row_query(q_ref, n_groups, scale):
    shape = (SUBLANES, BRANCH_W)
    width = BRANCH_W // n_groups
    own = lax.broadcasted_iota(jnp.int32, shape, 1) // width == lax.broadcasted_iota(jnp.int32, shape, 0)
    return jnp.where(own, jnp.broadcast_to(q_ref[...] * scale, shape), 0.0), own


def _gather_pages(kv_buf, slot, c, pg):
    return jnp.concatenate([kv_buf[slot, c, t] for t in range(pg)], axis=1).astype(BF16)


def _decode_softmax_step(s, vt, m_ref, l_ref, acc_ref):
    m_old = m_ref[:, 0:1]
    m_new = jnp.maximum(m_old, jnp.max(s, axis=1, keepdims=True))
    p = jnp.exp(s - m_new)
    alpha = jnp.exp(m_old - m_new)
    l_ref[:, 0:1] = alpha * l_ref[:, 0:1] + jnp.sum(p, axis=1, keepdims=True)
    m_ref[:, 0:1] = m_new
    acc_ref[...] = alpha * acc_ref[...] + _dot_nt(p.astype(BF16), vt)


def _decode_self_init(q8, kn_ref, vn_ref, m_ref, l_ref, acc_ref):
    m_ref[:, 0:1] = jnp.sum(q8 * kn_ref[...], axis=1, keepdims=True)
    l_ref[:, 0:1] = jnp.ones((SUBLANES, 1), F32)
    acc_ref[...] = jnp.broadcast_to(vn_ref[...], acc_ref.shape)


def _decode_kernel(pg, page_base, lam_init, pt_ref, fq_ref, dq_ref, sq_ref, fkn_ref, fvn_ref, dkn_ref, dvn_ref, lfn_ref,
                   lq1_ref, lk1_ref, lq2_ref, lk2_ref, lf_hbm, fk_hbm, fv_hbm, dk_hbm, dv_hbm, sk_hbm, sv_hbm,
                   fo_ref, do_ref, so_ref, fm_ref, fl_ref, facc_ref, fc_ref, lf_ref, dm_ref, dl_ref, dacc_ref,
                   sr_ref, sacc_ref, kv_buf, lf_buf, sem):
    steps = pl.num_programs(1)
    j = pl.program_id(1)
    first, last = j == 0, j == steps - 1
    n = pl.program_id(0) * steps + j
    slot = lax.rem(n, 2)

    def page_copies(step, into):
        b, group = lax.div(step, steps), steps - 1 - lax.rem(step, steps)
        copies = []
        for t in range(pg):
            page = page_base + pt_ref[b, group * pg + t]
            copies.append(pltpu.make_async_copy(lf_hbm.at[page], lf_buf.at[into, t], sem.at[into]))
            for c, hbm in enumerate((fk_hbm, fv_hbm, dk_hbm, dv_hbm, sk_hbm, sv_hbm)):
                copies.append(pltpu.make_async_copy(hbm.at[page], kv_buf.at[into, c, t], sem.at[into]))
        return copies

    @pl.when(n == 0)
    def _():
        for copy in page_copies(n, slot):
            copy.start()

    @pl.when(n + 1 < pl.num_programs(0) * steps)
    def _():
        for copy in page_copies(n + 1, 1 - slot):
            copy.start()

    for copy in page_copies(n, slot):
        copy.wait()

    fq8, f_own = _row_query(fq_ref, N_HEADS, ATTN_SCALE)
    dq8, _ = _row_query(dq_ref, 2 * N_HEADS, DIFF_SCALE)
    sq8, s_own = _row_query(sq_ref, N_HEADS, ATTN_SCALE)

    @pl.when(first)
    def _():
        _decode_self_init(fq8, fkn_ref, fvn_ref, fm_ref, fl_ref, facc_ref)
        _decode_self_init(dq8, dkn_ref, dvn_ref, dm_ref, dl_ref, dacc_ref)
        fc_ref[...] = jnp.zeros(fc_ref.shape, F32)
        fc_ref[0:N_HEADS, 0:1] = lfn_ref[...]
        lf_ref[...] = jnp.zeros(lf_ref.shape, F32)
        sr_ref[...] = jnp.zeros(sr_ref.shape, F32)
        sacc_ref[...] = jnp.zeros(sacc_ref.shape, F32)

    pages = functools.partial(_gather_pages, kv_buf, slot, pg=pg)
    for t in range(pg):
        lf_ref[0:N_HEADS, t * PAGE_SIZE:(t + 1) * PAGE_SIZE] = lf_buf[slot, t]
    lf = lf_ref[...]
    incl = _suffix_sum_lanes(lf)
    carry = fc_ref[:, 0:1]
    fc_ref[:, 0:1] = carry + incl[:, 0:1]
    s = _dot(fq8.astype(BF16), pages(c=0)) + (incl - lf + carry)
    _decode_softmax_step(s, pages(c=1), fm_ref, fl_ref, facc_ref)

    s = _dot(dq8.astype(BF16), pages(c=2))
    _decode_softmax_step(s, pages(c=3), dm_ref, dl_ref, dacc_ref)

    z = _dot(sq8.astype(BF16), pages(c=4))
    ls = _log_sigmoid(z)
    keep = ls - z
    incl = _suffix_sum_lanes(keep)
    r_old = sr_ref[:, 0:1]
    a = jnp.exp(ls + (incl - keep) + r_old)
    sr_ref[:, 0:1] = r_old + incl[:, 0:1]
    sacc_ref[...] += _dot_nt(a.astype(BF16), pages(c=5))

    @pl.when(last)
    def _():
        fo_ref[...] = jnp.sum(jnp.where(f_own, facc_ref[...] / fl_ref[:, 0:1], 0.0), axis=0, keepdims=True)
        lam = _diff_lambda(lq1_ref, lk1_ref, lq2_ref, lk2_ref, lam_init)
        shape = (SUBLANES, BRANCH_W)
        row = lax.broadcasted_iota(jnp.int32, shape, 0)
        head = lax.broadcasted_iota(jnp.int32, shape, 1) // HEAD_DIM
        o8 = dacc_ref[...] / dl_ref[:, 0:1] * jnp.where(row % 2 == 0, 1.0, -lam)
        do_ref[...] = jnp.sum(jnp.where(head == row // 2, o8, 0.0), axis=0, keepdims=True)
        so_ref[...] = jnp.sum(jnp.where(s_own, sacc_ref[...], 0.0), axis=0, keepdims=True)


def _decode_attention(page_table, page_base, queries, new_kv, lfn, lams, lf_cache, kv_caches, lam_init):
    db, n_pages = page_table.shape
    pg = PAGES_PER_STEP if n_pages % PAGES_PER_STEP == 0 else n_pages
    steps = n_pages // pg

    row = pl.BlockSpec((None, 1, BRANCH_W), lambda b, j, pt: (b, 0, 0))
    small = lambda shape: pl.BlockSpec(shape, lambda b, j, pt: (0,) * len(shape))
    in_hbm = pl.BlockSpec(memory_space=pl.ANY)
    in_specs = ([row] * 7 + [pl.BlockSpec((None, N_HEADS, 1), lambda b, j, pt: (b, 0, 0))] + [small((1, DIFF_DC))] * 4
                + [in_hbm] * 7)
    state = pltpu.VMEM((SUBLANES, LANES), F32)
    acc = pltpu.VMEM((SUBLANES, BRANCH_W), F32)
    scratch = [state, state, acc, state, pltpu.VMEM((SUBLANES, pg * PAGE_SIZE), F32), state, state, acc, state, acc,
               pltpu.VMEM((2, len(kv_caches), pg, BRANCH_W, PAGE_SIZE), F32),
               pltpu.VMEM((2, pg, N_HEADS, PAGE_SIZE), F32),
               pltpu.SemaphoreType.DMA((2,))]
    out = jax.ShapeDtypeStruct((db, 1, BRANCH_W), F32)
    return _call(functools.partial(_decode_kernel, pg, page_base, lam_init), grid=(db, steps), in_specs=in_specs,
                 out_specs=[row] * 3, out_shape=[out] * 3, scratch=scratch, prefetch=1,
                 name="decode_attn")(page_table, *queries, *new_kv, lfn, *lams, lf_cache, *kv_caches)


def _layer_params(l, g_mix_pre, g_mix_post, g_ffn_pre, g_ffn_post, g_mem, w_in, b_fox_f, g_diff, w_mem_k, w_mem_v,
                  w_branch, w_gate, b_gate, w_out, w_ffn_in, conv_w, conv_b, w_ffn_out):
    row = lambda a: a[l][None, :].astype(F32)
    wt = jnp.transpose(w_in[l]).astype(BF16)
    grp = lambda g: wt[g * BRANCH_W:(g + 1) * BRANCH_W]
    wft = jnp.zeros((SUBLANES, wt.shape[1]), BF16).at[:N_HEADS].set(wt[10 * BRANCH_W:])
    bf = jnp.zeros((SUBLANES, 1), F32).at[:N_HEADS, 0].set(b_fox_f[l])
    head = jnp.arange(BRANCH_W) // HEAD_DIM
    w_main = w_in[l][:, :10 * BRANCH_W].astype(BF16)
    cols = lambda g: w_main[:, g * BRANCH_W:(g + 1) * BRANCH_W]
    return dict(
        g_mix_pre=row(g_mix_pre), g_mix_post=row(g_mix_post), g_ffn_pre=row(g_ffn_pre), g_ffn_post=row(g_ffn_post),
        g_mem=row(g_mem), g_diff=jnp.tile(g_diff[l], N_HEADS)[None, :].astype(F32),
        wt=jnp.concatenate([grp(g) for g in (0, 3, 6, 9, 1, 2, 4, 5, 7, 8)], axis=0),
        w_all=jnp.concatenate([cols(g) for g in (0, 3, 6, 9, 1, 2, 4, 5, 7, 8)], axis=1),
        wft=wft, bf=bf,
        wmkt=jnp.transpose(w_mem_k[l]).astype(BF16), wmvt=jnp.transpose(w_mem_v[l]).astype(BF16),
        hsum=(head[:, None] == head[None, :]).astype(BF16),
        wb=w_branch[l].astype(BF16), wg=w_gate[l].astype(BF16), bg=row(b_gate), wo=w_out[l].astype(BF16),
        wfi=w_ffn_in[l].astype(BF16), cw=conv_w[l].astype(F32), cb=row(conv_b), wfo=w_ffn_out[l].astype(BF16))


def _feature_major_pages(cache):
    d, n, ps, h, e = cache.shape
    return jnp.transpose(cache, (0, 1, 3, 4, 2)).reshape(d * n, h * e, ps)


def _kv_out(a):
    d, b, _, s = a.shape
    return jnp.transpose(a.reshape(d, b, N_HEADS, HEAD_DIM, s), (0, 1, 4, 2, 3))


def kernel(x_prompt, mem_prompt, x_sample, cache_fox_k, cache_fox_v, cache_fox_logf, cache_diff_k, cache_diff_v,
           cache_sb_k, cache_sb_v, cache_mem_k, cache_mem_v, state_conv, page_table, g_mix_pre, g_mix_post,
           g_ffn_pre, g_ffn_post, g_mem, w_in, b_fox_f, diff_lq1, diff_lk1, diff_lq2, diff_lk2, g_diff, w_mem_k,
           w_mem_v, w_branch, w_gate, b_gate, w_out, w_ffn_in, conv_w, conv_b, w_ffn_out):
    depth = w_in.shape[0]
    batch, seq, d_model = x_prompt.shape
    db = x_sample.shape[0]
    n_pool = cache_fox_k.shape[1]
    past_len = page_table.shape[1] * PAGE_SIZE
    n_mem = cache_mem_k.shape[2]
    assert x_sample.shape[1] == 1 and seq % ATT_BLOCK == 0
    nq = seq // ATT_BLOCK

    kv_caches = [_feature_major_pages(c) for c in (cache_fox_k, cache_fox_v, cache_diff_k, cache_diff_v,
                                                   cache_sb_k, cache_sb_v)]
    lf_cache = jnp.transpose(cache_fox_logf, (0, 1, 3, 2)).reshape(depth * n_pool, N_HEADS, PAGE_SIZE)
    mem_kt = jnp.transpose(cache_mem_k, (0, 1, 3, 4, 2)).reshape(depth * db, BRANCH_W, n_mem)
    mem_vt = jnp.transpose(cache_mem_v, (0, 1, 3, 4, 2)).reshape(depth * db, BRANCH_W, n_mem)
    later = (jnp.arange(ATT_BLOCK)[:, None] < jnp.arange(ATT_BLOCK)[None, :]).astype(BF16)

    xp = x_prompt.reshape(batch * seq, d_model)
    xs = x_sample.reshape(db, d_model)
    p_rows, s_rows, kv_stacks = [], [], None
    for l in range(depth):
        lam_init = 0.8 - 0.6 * math.exp(-0.3 * l)
        p = _layer_params(l, g_mix_pre, g_mix_post, g_ffn_pre, g_ffn_post, g_mem, w_in, b_fox_f, g_diff, w_mem_k,
                          w_mem_v, w_branch, w_gate, b_gate, w_out, w_ffn_in, conv_w, conv_b, w_ffn_out)
        lams = [a[l][None, :].astype(F32) for a in (diff_lq1, diff_lk1, diff_lq2, diff_lk2)]

        fqt, dqt, sqt, mqt, *kv_stacks, lft = _project_prompt(xp, batch, seq, p, l, depth, kv_stacks)
        fkt, fvt, dkt, dvt, skt, svt = kv_stacks
        e = _fox_suffix(lft)
        ecol = jnp.transpose(e[:, :N_HEADS, :], (0, 2, 1))
        erow = jnp.transpose(e.reshape(batch, SUBLANES, nq, ATT_BLOCK), (0, 2, 1, 3))
        fox_o = _prompt_attention("fox", l, fqt, fkt, fvt, (ecol, erow))
        diff_o = _prompt_attention("diff", l, dqt, dkt, dvt, tuple(lams), lam_init)
        sb_o = _prompt_attention("sb", l, sqt, skt, svt, (later,))
        mkt, mvt = _mem_kv(mem_prompt, p)
        mem_o = _mem_attention_prompt(mqt, mkt, mvt)
        flat = lambda a: a.reshape(batch * seq, BRANCH_W)
        hp = _merge(xp, flat(fox_o), flat(diff_o), flat(sb_o), flat(mem_o), p, lam_init)
        xp, conv_p = _ffn_prompt(hp, batch, seq, p)
        p_rows.append((lft[:, :N_HEADS, :], mkt, mvt, conv_p))

        sfq, sdq, ssq, smq, sfk, sfv, sdk, sdv, ssk, ssv, slf = _project_decode(xs, past_len, p)
        one = lambda a: a.reshape(db, 1, BRANCH_W)
        slf4 = jnp.transpose(slf[:N_HEADS, :])
        fox_s, diff_s, sb_s = _decode_attention(
            page_table, l * n_pool, (one(sfq), one(sdq), one(ssq)), (one(sfk), one(sfv), one(sdk), one(sdv)),
            slf4[:, :, None], lams, lf_cache, kv_caches, lam_init)
        smq8 = jnp.zeros((db, SUBLANES, BRANCH_W), F32).at[:, 0, :].set(smq)
        mem_s = _mem_attention(smq8, mem_kt, mem_vt, SUBLANES, l * db)[:, 0, :]
        two = lambda a: a.reshape(db, BRANCH_W)
        hs = _merge(xs, two(fox_s), two(diff_s), two(sb_s), mem_s, p, lam_init)
        xs, conv_s = _ffn_decode(hs, state_conv[l], p)
        heads = lambda a: a.reshape(db, 1, N_HEADS, HEAD_DIM)
        s_rows.append((heads(sfk), heads(sfv), slf4[:, None, :], heads(sdk), heads(sdv), heads(ssk), heads(ssv), conv_s))

    p_fk, p_fv, p_dk, p_dv, p_sk, p_sv = (_kv_out(a) for a in kv_stacks)
    p_fl, p_mk, p_mv, p_cv = (jnp.stack(a) for a in zip(*p_rows))
    p_fl = jnp.transpose(p_fl, (0, 1, 3, 2))
    p_mk, p_mv = _kv_out(p_mk), _kv_out(p_mv)
    s_out = [jnp.stack(a) for a in zip(*s_rows)]
    return (xp.reshape(batch, seq, d_model), xs.reshape(db, 1, d_model), p_fk, p_fv, p_fl, p_dk, p_dv, p_sk, p_sv,
            p_mk, p_mv, p_cv, *s_out)
```
